```python
import jax, jax.numpy as jnp
from jax import lax
import numpy as np

D_MODEL = 4096
BATCH = 2
SEQ = 4096
DEPTH = 4
DEC_BATCH = 8
DEC_SEQ = 32
PAST_LEN = 1024

CHUNK = 64
Q_BLOCK = 128
N_A_LAYERS = DEPTH // 2
N_B_LAYERS = DEPTH - N_A_LAYERS
EXPAND = 128
H_A = D_MODEL // EXPAND
DK_A = EXPAND
DV_A = D_MODEL // H_A
H_B = 32
D_HEAD_B = D_MODEL // H_B
SB_SCALE = D_HEAD_B ** -0.5
D_FF = -(-8 * D_MODEL // (3 * 256)) * 256
EPS = 1e-6

kernel_name = 'hybrid_hgrn2_stickbreak_yoco_step'


def rms_norm(x, g):
    xf = x.astype(jnp.float32)
    y = xf * lax.rsqrt(jnp.mean(xf * xf, axis=-1, keepdims=True) + EPS)
    return (y * g.astype(jnp.float32)).astype(x.dtype)


def swiglu(x, w_gate_up, w_down):
    gate, up = jnp.split(x @ w_gate_up, 2, axis=-1)
    return (jax.nn.silu(gate) * up) @ w_down


def hgrn2_recurrence(q, k, v, log_f, s0):
    B, S, H, _ = q.shape
    DV = v.shape[-1]
    n_chunks = -(-S // CHUNK)
    pad = n_chunks * CHUNK - S

    def prep(a):
        a = jnp.pad(a.astype(jnp.float32), ((0, 0), (0, pad), (0, 0), (0, 0)))
        return a.reshape(B, n_chunks, CHUNK, H, a.shape[-1]).transpose(1, 0, 3, 2, 4)

    qc, kc, vc, gc = prep(q), prep(k), prep(v), prep(log_f)
    causal = jnp.tril(jnp.ones((CHUNK, CHUNK), bool))[:, :, None]

    def step(state, inp):
        qb, kb, vb, gb = inp
        b = jnp.cumsum(gb, axis=2)
        diff = b[:, :, :, None, :] - b[:, :, None, :, :]
        decay = jnp.exp(jnp.where(causal, diff, -jnp.inf))
        scores = jnp.einsum('bhtc,bhsc,bhtsc->bhts', qb, kb, decay)
        o = scores @ vb + jnp.einsum('bhtc,bhcv->bhtv', qb * jnp.exp(b), state)
        b_last = b[:, :, -1:, :]
        k_dec = kb * jnp.exp(b_last - b)
        new_state = jnp.exp(b_last[:, :, 0, :])[..., None] * state + jnp.einsum('bhsc,bhsv->bhcv', k_dec, vb)
        return new_state, o

    s_final, o = lax.scan(step, s0.astype(jnp.float32), (qc, kc, vc, gc))
    o = o.transpose(1, 0, 3, 2, 4).reshape(B, n_chunks * CHUNK, H, DV)[:, :S]
    return o, s_final


def hgrn2_mixer(h, w_in, lb, g_norm, w_out, s0):
    B, S, _ = h.shape
    q, f, i, g = jnp.split(h @ w_in, 4, axis=-1)
    forget = lb + (1.0 - lb) * jax.nn.sigmoid(f.astype(jnp.float32))
    log_f = jnp.log(forget)
    k = 1.0 - forget
    split_heads = lambda a, d: a.reshape(B, S, H_A, d)
    o, s_new = hgrn2_recurrence(split_heads(jax.nn.silu(q), DK_A), split_heads(k, DK_A),
                                split_heads(i, DV_A), split_heads(log_f, DK_A), s0)
    o = rms_norm(o, g_norm.reshape(H_A, DV_A)).reshape(B, S, D_MODEL).astype(h.dtype)
    return (o * jax.nn.silu(g)) @ w_out, s_new.astype(h.dtype)


def stick_breaking_block(q, k, v, q_pos, k_pos):
    z = jnp.einsum('bhtd,bhsd->bhts', q, k, preferred_element_type=jnp.float32) * SB_SCALE
    mask = k_pos[None, :] < q_pos[:, None]
    log_keep = jnp.where(mask, jax.nn.log_sigmoid(-z), 0.0)
    after = lax.cumsum(log_keep, axis=3, reverse=True) - log_keep
    w = jnp.where(mask, jnp.exp(jax.nn.log_sigmoid(z) + after), 0.0)
    return jnp.einsum('bhts,bhsd->bhtd', w.astype(v.dtype), v)


def stick_breaking_prompt(q, k, v):
    B, H, S, d = q.shape
    nb = S // Q_BLOCK
    q_blocks = q.reshape(B, H, nb, Q_BLOCK, d).transpose(2, 0, 1, 3, 4)
    pos = jnp.arange(S)
    q_pos = pos.reshape(nb, Q_BLOCK)
    o = lax.map(lambda a: stick_breaking_block(a[0], k, v, a[1], pos), (q_blocks, q_pos))
    return o.transpose(1, 2, 0, 3, 4).reshape(B, H, S, d)


def run_trunk(x, s0, k_past, v_past, norm_mix, norm_ffn, norm_kv, norm_out, a_w_in, lb, a_gnorm,
              a_w_out, b_w_kv, b_w_q, b_w_out, ffn_w_gate_up, ffn_w_down):
    B, S, _ = x.shape
    new_states = []
    for layer in range(DEPTH):
        h = rms_norm(x, norm_mix[layer])
        if layer < N_A_LAYERS:
            y, s_new = hgrn2_mixer(h, a_w_in[layer], lb[layer], a_gnorm[layer], a_w_out[layer], s0[layer])
            new_states.append(s_new)
        else:
            if layer == N_A_LAYERS:
                k_new, v_new = jnp.split(rms_norm(x, norm_kv) @ b_w_kv, 2, axis=-1)
                k_new = k_new.reshape(B, S, H_B, D_HEAD_B)
                v_new = v_new.reshape(B, S, H_B, D_HEAD_B)
                if k_past is None:
                    k_all, v_all = k_new, v_new
                else:
                    k_all = jnp.concatenate([k_past.astype(k_new.dtype), k_new], axis=1)
                    v_all = jnp.concatenate([v_past.astype(v_new.dtype), v_new], axis=1)
                k_heads = k_all.transpose(0, 2, 1, 3)
                v_heads = v_all.transpose(0, 2, 1, 3)
            j = layer - N_A_LAYERS
            q = (h @ b_w_q[j]).reshape(B, S, H_B, D_HEAD_B).transpose(0, 2, 1, 3)
            if k_past is None:
                o = stick_breaking_prompt(q, k_heads, v_heads)
            else:
                past = k_past.shape[1]
                o = stick_breaking_block(q, k_heads, v_heads, past + jnp.arange(S), jnp.arange(past + S))
            y = o.transpose(0, 2, 1, 3).reshape(B, S, D_MODEL).astype(x.dtype) @ b_w_out[j]
        x = x + y
        x = x + swiglu(rms_norm(x, norm_ffn[layer]), ffn_w_gate_up[layer], ffn_w_down[layer])
    return rms_norm(x, norm_out), jnp.stack(new_states), k_new, v_new


def setup_inputs(seed: int = 0) -> dict:
    key = jax.random.key(seed)
    ks = jax.random.split(key, 18)
    f32 = jnp.float32

    def normal(k, shape):
        return jax.random.normal(k, shape, f32)

    def dense(k, shape, fan_in):
        return normal(k, shape) * (fan_in ** -0.5)

    def gain(k, shape):
        return 1.0 + 0.01 * normal(k, shape)

    return {
        'x_prompt': normal(ks[0], (BATCH, SEQ, D_MODEL)),
        'x_sample': normal(ks[1], (DEC_BATCH, DEC_SEQ, D_MODEL)),
        'state_hgrn': 0.5 * normal(ks[2], (N_A_LAYERS, DEC_BATCH, H_A, DK_A, DV_A)),
        'cache_k': normal(ks[3], (DEC_BATCH, PAST_LEN, H_B, D_HEAD_B)),
        'cache_v': normal(ks[4], (DEC_BATCH, PAST_LEN, H_B, D_HEAD_B)),
        'norm_mix': gain(ks[5], (DEPTH, D_MODEL)),
        'norm_ffn': gain(ks[6], (DEPTH, D_MODEL)),
        'norm_kv': gain(ks[7], (D_MODEL,)),
        'norm_out': gain(ks[8], (D_MODEL,)),
        'a_w_in': dense(ks[9], (N_A_LAYERS, D_MODEL, 4 * D_MODEL), D_MODEL),
        'a_lb_logits': 0.5 * normal(ks[10], (N_A_LAYERS, D_MODEL)),
        'a_gnorm': gain(ks[11], (N_A_LAYERS, D_MODEL)),
        'a_w_out': dense(ks[12], (N_A_LAYERS, D_MODEL, D_MODEL), D_MODEL),
        'b_w_kv': dense(ks[13], (D_MODEL, 2 * D_MODEL), D_MODEL),
        'b_w_q': dense(ks[14], (N_B_LAYERS, D_MODEL, D_MODEL), D_MODEL),
        'b_w_out': dense(ks[15], (N_B_LAYERS, D_MODEL, D_MODEL), D_MODEL),
        'ffn_w_gate_up': dense(ks[16], (DEPTH, D_MODEL, 2 * D_FF), D_MODEL),
        'ffn_w_down': dense(ks[17], (DEPTH, D_FF, D_MODEL), D_FF),
    }


def reference(x_prompt, x_sample, state_hgrn, cache_k, cache_v, norm_mix, norm_ffn, norm_kv, norm_out,
              a_w_in, a_lb_logits, a_gnorm, a_w_out, b_w_kv, b_w_q, b_w_out, ffn_w_gate_up, ffn_w_down):
    lb = jnp.cumsum(jax.nn.softmax(a_lb_logits.astype(jnp.float32), axis=0), axis=0)
    lb = lb - lb[0:1]
    s0_prompt = jnp.zeros((N_A_LAYERS, x_prompt.shape[0], H_A, DK_A, DV_A), x_prompt.dtype)
    y_prompt, s_prompt, k_prompt, v_prompt = run_trunk(
        x_prompt, s0_prompt, None, None, norm_mix, norm_ffn, norm_kv, norm_out, a_w_in, lb, a_gnorm,
        a_w_out, b_w_kv, b_w_q, b_w_out, ffn_w_gate_up, ffn_w_down)
    y_sample, s_sample, k_sample, v_sample = run_trunk(
        x_sample, state_hgrn, cache_k, cache_v, norm_mix, norm_ffn, norm_kv, norm_out, a_w_in, lb, a_gnorm,
        a_w_out, b_w_kv, b_w_q, b_w_out, ffn_w_gate_up, ffn_w_down)
    return (y_prompt, y_sample, s_prompt, k_prompt, v_prompt, s_sample, k_sample, v_sample)
```

```python
import functools

import numpy as np
import jax
import jax.numpy as jnp
from jax import lax
from jax.experimental import pallas as pl
from jax.experimental.pallas import tpu as pltpu

F32 = jnp.float32
BF16 = jnp.bfloat16
EPS = 1e-6
HEAD = 128
HGRN_CHUNK = 64
SB_TILE = 256
V7X_VMEM_LIMIT_BYTES = 56 * 1024 * 1024

_NT = (((1,), (1,)), ((), ()))
_TN = (((0,), (0,)), ((), ()))


def _params(*sem):
    return pltpu.CompilerParams(dimension_semantics=sem, vmem_limit_bytes=V7X_VMEM_LIMIT_BYTES)


def _rmsnorm_kernel(x_ref, g_ref, o_ref):
    x = x_ref[...]
    y = x * lax.rsqrt(jnp.mean(x * x, axis=-1, keepdims=True) + EPS)
    o_ref[...] = (y * g_ref[...]).astype(o_ref.dtype)


def rmsnorm(x, g, out_dtype, rows):
    m, d = x.shape
    return pl.pallas_call(
        _rmsnorm_kernel,
        grid=(m // rows,),
        in_specs=[pl.BlockSpec((rows, d), lambda i: (i, 0)), pl.BlockSpec((1, d), lambda i: (0, 0))],
        out_specs=pl.BlockSpec((rows, d), lambda i: (i, 0)),
        out_shape=jax.ShapeDtypeStruct((m, d), out_dtype),
        compiler_params=_params("arbitrary"),
        name="rmsnorm",
    )(x, g.reshape(1, d))


def _matmul_kernel(*refs, has_res, n_out):
    a_ref, w_ref = refs[0], refs[1]
    acc = jnp.dot(a_ref[...], w_ref[...], preferred_element_type=F32)
    if has_res:
        acc = refs[2][...] + acc
    for o_ref in refs[len(refs) - n_out:]:
        o_ref[...] = acc.astype(o_ref.dtype)


def matmul(a, w, res=None, out_dtypes=(F32,), tm=1056, tn=512):
    m, k = a.shape
    n = w.shape[1]
    assert m % tm == 0 and n % tn == 0
    in_specs = [pl.BlockSpec((tm, k), lambda i, j: (i, 0)), pl.BlockSpec((k, tn), lambda i, j: (0, j))]
    args = [a, w]
    if res is not None:
        in_specs.append(pl.BlockSpec((tm, tn), lambda i, j: (i, j)))
        args.append(res)
    outs = pl.pallas_call(
        functools.partial(_matmul_kernel, has_res=res is not None, n_out=len(out_dtypes)),
        grid=(m // tm, n // tn),
        in_specs=in_specs,
        out_specs=[pl.BlockSpec((tm, tn), lambda i, j: (i, j)) for _ in out_dtypes],
        out_shape=[jax.ShapeDtypeStruct((m, n), dt) for dt in out_dtypes],
        compiler_params=_params("arbitrary", "arbitrary"),
        name="matmul",
    )(*args)
    return outs[0] if len(out_dtypes) == 1 else outs


def _gate_up_kernel(a_ref, wg_ref, wu_ref, o_ref):
    a = a_ref[...]
    gate = jnp.dot(a, wg_ref[...], preferred_element_type=F32)
    up = jnp.dot(a, wu_ref[...], preferred_element_type=F32)
    o_ref[...] = (gate * jax.nn.sigmoid(gate) * up).astype(o_ref.dtype)


def gate_up(a, w, tm=1056, tn=256):
    m, k = a.shape
    f = w.shape[1] // 2
    assert m % tm == 0 and f % tn == 0
    nj = f // tn
    return pl.pallas_call(
        _gate_up_kernel,
        grid=(m // tm, nj),
        in_specs=[pl.BlockSpec((tm, k), lambda i, j: (i, 0)),
                  pl.BlockSpec((k, tn), lambda i, j: (0, j)),
                  pl.BlockSpec((k, tn), lambda i, j: (0, j + nj))],
        out_specs=pl.BlockSpec((tm, tn), lambda i, j: (i, j)),
        out_shape=jax.ShapeDtypeStruct((m, f), BF16),
        compiler_params=_params("arbitrary", "arbitrary"),
        name="gate_up",
    )(a, w, w)


def _hgrn_sum_matrix(c):
    r = np.arange(c)[:, None]
    u = np.arange(c)[None, :]
    blocks = []
    n = c // 2
    while n >= 1:
        m = (r // (2 * n)) * (2 * n) + n - 1
        blocks.append(np.where((r & n) != 0, (u > m) & (u <= r), (u > r) & (u <= m)))
        n //= 2
    blocks.append(u <= r)
    blocks.append(u > r)
    return np.concatenate(blocks, axis=0).astype(np.float32)


def _hgrn_chunk(qz, fz, iz, gz, lb, gn, p, st, c):
    n_levels = c.bit_length() - 1
    q = qz * jax.nn.sigmoid(qz)
    forget = lb + (1.0 - lb) * jax.nn.sigmoid(fz)
    log_f = jnp.log(forget)
    k = 1.0 - forget
    v16 = iz.astype(BF16)

    hi = log_f.astype(BF16)
    rem = log_f - hi.astype(F32)
    mid = rem.astype(BF16)
    lo = (rem - mid.astype(F32)).astype(BF16)
    sums3 = jnp.dot(p, jnp.concatenate([hi, mid, lo], axis=1), preferred_element_type=F32)
    sums = sums3[:, :HEAD] + sums3[:, HEAD:2 * HEAD] + sums3[:, 2 * HEAD:]
    decay = jnp.exp(sums)

    t_idx = lax.broadcasted_iota(jnp.int32, (c, c), 0)
    s_idx = lax.broadcasted_iota(jnp.int32, (c, c), 1)
    x_idx = t_idx ^ s_idx
    scores = jnp.zeros((c, c), F32)
    for lvl in reversed(range(n_levels)):
        n = c >> (lvl + 1)
        e = decay[lvl * c:(lvl + 1) * c]
        s_l = lax.dot_general((q * e).astype(BF16), (k * e).astype(BF16), _NT, preferred_element_type=F32)
        scores = jnp.where(x_idx >= n, s_l, scores)
    diag = jnp.where(t_idx == s_idx, jnp.sum(q * k, axis=-1, keepdims=True), 0.0)
    scores = jnp.where(t_idx > s_idx, scores, diag)

    e_in = decay[n_levels * c:(n_levels + 1) * c]
    e_out = decay[(n_levels + 1) * c:]
    o = jnp.dot(scores.astype(BF16), v16, preferred_element_type=F32)
    o = o + lax.dot_general((q * e_in).astype(BF16), st.astype(BF16), _NT, preferred_element_type=F32)
    st_new = st * e_in[c - 1:c, :] + lax.dot_general(v16, (k * e_out).astype(BF16), _TN,
                                                     preferred_element_type=F32)

    o = o * lax.rsqrt(jnp.mean(o * o, axis=-1, keepdims=True) + EPS) * gn
    return o * (gz * jax.nn.sigmoid(gz)), st_new


def _hgrn_kernel(q_ref, f_ref, i_ref, g_ref, lbl_ref, gn_ref, p_ref, s0_ref, o_ref, s_ref, st_ref,
                 *, layer, chunk, n_chunks):
    step = pl.program_id(2)

    @pl.when(step == 0)
    def _():
        st_ref[...] = s0_ref[0, 0].T

    logits = lbl_ref[...]
    ex = jnp.exp(logits - jnp.max(logits, axis=0, keepdims=True))
    sm = ex / jnp.sum(ex, axis=0, keepdims=True)
    lb = jnp.zeros_like(sm[0:1])
    for r in range(1, layer + 1):
        lb = lb + sm[r:r + 1]
    gn = gn_ref[...]
    p = p_ref[...]

    def body(ci, carry):
        rows = pl.ds(pl.multiple_of(ci * chunk, chunk), chunk)
        o, st_new = _hgrn_chunk(q_ref[rows, :], f_ref[rows, :], i_ref[rows, :], g_ref[rows, :],
                                lb, gn, p, st_ref[...], chunk)
        o_ref[rows, :] = o.astype(o_ref.dtype)
        st_ref[...] = st_new
        return carry

    lax.fori_loop(0, n_chunks, body, 0)

    @pl.when(step == pl.num_programs(2) - 1)
    def _():
        s_ref[0, 0] = st_ref[...].T.astype(s_ref.dtype)


def hgrn_mixer(z, lb_logits, gnorm, s0, layer, row0, seq, chunk, rows_per_step):
    d = z.shape[1] // 4
    heads = d // HEAD
    batch = s0.shape[0]
    steps = seq // rows_per_step
    blk0 = row0 // rows_per_step
    assert row0 % rows_per_step == 0 and seq % rows_per_step == 0 and rows_per_step % chunk == 0
    p = jnp.asarray(_hgrn_sum_matrix(chunk), BF16)

    def col(part):
        return pl.BlockSpec((rows_per_step, HEAD), lambda b, h, s: (blk0 + b * steps + s, part * heads + h))

    head_row = lambda b, h, s: (0, h)
    state_blk = pl.BlockSpec((1, 1, HEAD, HEAD), lambda b, h, s: (b, h, 0, 0))
    return pl.pallas_call(
        functools.partial(_hgrn_kernel, layer=layer, chunk=chunk, n_chunks=rows_per_step // chunk),
        grid=(batch, heads, steps),
        in_specs=[col(0), col(1), col(2), col(3),
                  pl.BlockSpec((lb_logits.shape[0], HEAD), head_row),
                  pl.BlockSpec((1, HEAD), head_row),
                  pl.BlockSpec(p.shape, lambda b, h, s: (0, 0)),
                  state_blk],
        out_specs=[pl.BlockSpec((rows_per_step, HEAD), lambda b, h, s: (b * steps + s, h)), state_blk],
        out_shape=[jax.ShapeDtypeStruct((batch * seq, d), BF16),
                   jax.ShapeDtypeStruct(s0.shape, F32)],
        scratch_shapes=[pltpu.VMEM((HEAD, HEAD), F32)],
        compiler_params=_params("arbitrary", "arbitrary", "arbitrary"),
        name="hgrn_mixer",
    )(z, z, z, z, lb_logits, gnorm.reshape(1, d), p, s0)


def _later_keys_matrix(tk):
    j = lax.broadcasted_iota(jnp.int32, (tk, tk), 0)
    s = lax.broadcasted_iota(jnp.int32, (tk, tk), 1)
    return jnp.where(j > s, 1.0, 0.0).astype(BF16)


def _sb_tile(q, ks, vs, later, acc, run, scale, causal):
    z = lax.dot_general(q, ks, _NT, preferred_element_type=F32) * scale
    log_beta = jnp.minimum(z, 0.0) - jnp.log(1.0 + jnp.exp(-jnp.abs(z)))
    log_keep = log_beta - z
    if causal:
        mask = (lax.broadcasted_iota(jnp.int32, z.shape, 1) < lax.broadcasted_iota(jnp.int32, z.shape, 0))
        log_keep = jnp.where(mask, log_keep, 0.0)
    hi = log_keep.astype(BF16)
    lo = (log_keep - hi.astype(F32)).astype(BF16)
    after = (jnp.dot(hi, later, preferred_element_type=F32) + jnp.dot(lo, later, preferred_element_type=F32))
    w = jnp.exp(log_beta + after + run)
    if causal:
        w = jnp.where(mask, w, 0.0)
    acc = acc + jnp.dot(w.astype(BF16), vs, preferred_element_type=F32)
    run = run + jnp.sum(log_keep, axis=-1, keepdims=True)
    return acc, run


def _sb_prompt_kernel(q_ref, k_ref, v_ref, o_ref, *, tile, scale):
    qi = pl.program_id(2)
    q = q_ref[...]
    later = _later_keys_matrix(tile)

    def visit(j, carry, causal):
        rows = pl.ds(pl.multiple_of(j * tile, tile), tile)
        return _sb_tile(q, k_ref[rows, :], v_ref[rows, :], later, carry[0], carry[1], scale, causal)

    carry = (jnp.zeros((tile, HEAD), F32), jnp.zeros((tile, 1), F32))
    carry = visit(qi, carry, True)
    carry = lax.fori_loop(0, qi, lambda t, c: visit(qi - 1 - t, c, False), carry)
    o_ref[...] = carry[0].astype(o_ref.dtype)


def sb_prompt(q, kv16, batch, seq, tile=SB_TILE):
    d = q.shape[1]
    heads = d // HEAD
    nq = seq // tile
    return pl.pallas_call(
        functools.partial(_sb_prompt_kernel, tile=tile, scale=HEAD ** -0.5),
        grid=(batch, heads, nq),
        in_specs=[pl.BlockSpec((tile, HEAD), lambda b, h, i: (b * nq + i, h)),
                  pl.BlockSpec((seq, HEAD), lambda b, h, i: (b, h)),
                  pl.BlockSpec((seq, HEAD), lambda b, h, i: (b, heads + h))],
        out_specs=pl.BlockSpec((tile, HEAD), lambda b, h, i: (b * nq + i, h)),
        out_shape=jax.ShapeDtypeStruct((batch * seq, d), BF16),
        compiler_params=_params("arbitrary", "arbitrary", "arbitrary"),
        name="sb_prompt",
    )(q, kv16, kv16)


def _sb_sample_kernel(q_ref, k_ref, v_ref, ck_ref, cv_ref, o_ref, *, tile, scale):
    q = q_ref[...]
    t = q.shape[0]
    past = ck_ref.shape[0]
    carry = (jnp.zeros((t, HEAD), F32), jnp.zeros((t, 1), F32))
    carry = _sb_tile(q, k_ref[...], v_ref[...], _later_keys_matrix(t), carry[0], carry[1], scale, True)
    later = _later_keys_matrix(tile)
    for j in range(past // tile - 1, -1, -1):
        rows = slice(j * tile, (j + 1) * tile)
        carry = _sb_tile(q, ck_ref[rows, :].astype(BF16), cv_ref[rows, :].astype(BF16), later,
                         carry[0], carry[1], scale, False)
    o_ref[...] = carry[0].astype(o_ref.dtype)


def sb_sample(q, kv16, cache_k, cache_v, row0, batch, seq, tile=SB_TILE):
    d = q.shape[1]
    heads = d // HEAD
    past = cache_k.shape[1]
    blk0 = row0 // seq
    assert row0 % seq == 0 and past % tile == 0
    cache_blk = pl.BlockSpec((None, past, HEAD), lambda b, h: (b, 0, h))
    return pl.pallas_call(
        functools.partial(_sb_sample_kernel, tile=tile, scale=HEAD ** -0.5),
        grid=(batch, heads),
        in_specs=[pl.BlockSpec((seq, HEAD), lambda b, h: (blk0 + b, h)),
                  pl.BlockSpec((seq, HEAD), lambda b, h: (blk0 + b, h)),
                  pl.BlockSpec((seq, HEAD), lambda b, h: (blk0 + b, heads + h)),
                  cache_blk, cache_blk],
        out_specs=pl.BlockSpec((seq, HEAD), lambda b, h: (b, h)),
        out_shape=jax.ShapeDtypeStruct((batch * seq, d), BF16),
        compiler_params=_params("arbitrary", "arbitrary"),
        name="sb_sample",
    )(q, kv16, kv16, cache_k.reshape(batch, past, d), cache_v.reshape(batch, past, d))


def kernel(x_prompt, x_sample, state_hgrn, cache_k, cache_v, norm_mix, norm_ffn, norm_kv, norm_out,
           a_w_in, a_lb_logits, a_gnorm, a_w_out, b_w_kv, b_w_q, b_w_out, ffn_w_gate_up, ffn_w_down):
    pb, ps, d = x_prompt.shape
    sb, ss, _ = x_sample.shape
    n_a = a_w_in.shape[0]
    depth = norm_mix.shape[0]
    heads = d // HEAD
    mp, ms = pb * ps, sb * ss
    norm_rows = 384

    x = jnp.concatenate([x_prompt.reshape(mp, d), x_sample.reshape(ms, d)], axis=0)
    zero_state = jnp.zeros((pb, heads, HEAD, HEAD), F32)
    states_p, states_s = [], []
    kv32 = kv16 = None
    for layer in range(depth):
        h = rmsnorm(x, norm_mix[layer], BF16, norm_rows)
        if layer < n_a:
            z = matmul(h, a_w_in[layer].astype(BF16))
            o_p, s_p = hgrn_mixer(z, a_lb_logits, a_gnorm[layer], zero_state, layer, 0, ps,
                                  HGRN_CHUNK, 8 * HGRN_CHUNK)
            o_s, s_s = hgrn_mixer(z, a_lb_logits, a_gnorm[layer], state_hgrn[layer], layer, mp, ss, ss, ss)
            states_p.append(s_p)
            states_s.append(s_s)
            x = matmul(jnp.concatenate([o_p, o_s], axis=0), a_w_out[layer].astype(BF16), res=x)
        else:
            j = layer - n_a
            if j == 0:
                kv32, kv16 = matmul(rmsnorm(x, norm_kv, BF16, norm_rows), b_w_kv.astype(BF16),
                                    out_dtypes=(F32, BF16))
            q = matmul(h, b_w_q[j].astype(BF16), out_dtypes=(BF16,))
            o_p = sb_prompt(q, kv16, pb, ps)
            o_s = sb_sample(q, kv16, cache_k, cache_v, mp, sb, ss)
            x = matmul(jnp.concatenate([o_p, o_s], axis=0), b_w_out[j].astype(BF16), res=x)
        hidden = gate_up(rmsnorm(x, norm_ffn[layer], BF16, norm_rows), ffn_w_gate_up[layer].astype(BF16))
        x = matmul(hidden, ffn_w_down[layer].astype(BF16), res=x, tm=528, tn=256)
    y = rmsnorm(x, norm_out, F32, norm_rows)

    k_new, v_new = kv32[:, :d], kv32[:, d:]
    return (y[:mp].reshape(pb, ps, d), y[mp:].reshape(sb, ss, d),
            jnp.stack(states_p).astype(x.dtype), k_new[:mp].reshape(pb, ps, heads, HEAD),
            v_new[:mp].reshape(pb, ps, heads, HEAD),
            jnp.stack(states_s).astype(x.dtype), k_new[mp:].reshape(sb, ss, heads, HEAD),
            v_new[mp:].reshape(sb, ss, heads, HEAD))
```

```python
import functools

import numpy as np
import jax
import jax.numpy as jnp
from jax import lax
from jax.experimental import pallas as pl
from jax.experimental.pallas import tpu as pltpu

F32 = jnp.float32
BF16 = jnp.bfloat16
EPS = 1e-6
HEAD = 128
HGRN_CHUNK = 64
SB_KEY_TILE = 256
SB_QUERY_TILE = 2 * SB_KEY_TILE
SB_LOG2_SCALE = float(HEAD ** -0.5 * np.log2(np.e))
V7X_VMEM_LIMIT_BYTES = 56 * 1024 * 1024

_NT = (((1,), (1,)), ((), ()))
_TN = (((0,), (0,)), ((), ()))


def _params(*sem):
    return pltpu.CompilerParams(dimension_semantics=sem, vmem_limit_bytes=V7X_VMEM_LIMIT_BYTES)


def _rmsnorm_kernel(x_ref, g_ref, o_ref):
    x = x_ref[...]
    y = x * lax.rsqrt(jnp.mean(x * x, axis=-1, keepdims=True) + EPS)
    o_ref[...] = (y * g_ref[...]).astype(o_ref.dtype)


def rmsnorm(x, g, out_dtype, rows, row0=0, nrows=None):
    d = x.shape[1]
    nrows = x.shape[0] if nrows is None else nrows
    assert nrows % rows == 0 and row0 % rows == 0
    blk0 = row0 // rows
    return pl.pallas_call(
        _rmsnorm_kernel,
        grid=(nrows // rows,),
        in_specs=[pl.BlockSpec((rows, d), lambda i: (blk0 + i, 0)), pl.BlockSpec((1, d), lambda i: (0, 0))],
        out_specs=pl.BlockSpec((rows, d), lambda i: (i, 0)),
        out_shape=jax.ShapeDtypeStruct((nrows, d), out_dtype),
        compiler_params=_params("arbitrary"),
        name="rmsnorm",
    )(x, g.reshape(1, d))


def _mxu_operand(w_ref):
    w = w_ref[...]
    return w if w.dtype == BF16 else w.astype(BF16)


def _matmul_kernel(*refs, has_res, n_out, out_scale):
    a_ref, w_ref = refs[0], refs[1]
    acc = jnp.dot(a_ref[...], _mxu_operand(w_ref), preferred_element_type=F32)
    if out_scale is not None:
        acc = acc * out_scale
    if has_res:
        acc = refs[2][...] + acc
    for o_ref in refs[len(refs) - n_out:]:
        o_ref[...] = acc.astype(o_ref.dtype)


def _resident_rows(tm, k):
    return pl.BlockSpec((tm, k), lambda i, j: (i, 0), pipeline_mode=pl.Buffered(1))


def matmul(a, w, res=None, out_dtypes=(F32,), out_scale=None, tm=2112, tn=256):
    m, k = a.shape
    n = w.shape[1]
    assert m % tm == 0 and n % tn == 0
    in_specs = [_resident_rows(tm, k), pl.BlockSpec((k, tn), lambda i, j: (0, j))]
    args = [a, w]
    if res is not None:
        in_specs.append(pl.BlockSpec((tm, tn), lambda i, j: (i, j)))
        args.append(res)
    outs = pl.pallas_call(
        functools.partial(_matmul_kernel, has_res=res is not None, n_out=len(out_dtypes), out_scale=out_scale),
        grid=(m // tm, n // tn),
        in_specs=in_specs,
        out_specs=[pl.BlockSpec((tm, tn), lambda i, j: (i, j)) for _ in out_dtypes],
        out_shape=[jax.ShapeDtypeStruct((m, n), dt) for dt in out_dtypes],
        compiler_params=_params("arbitrary", "arbitrary"),
        name="matmul",
    )(*args)
    return outs[0] if len(out_dtypes) == 1 else outs


def _gate_up_kernel(a_ref, wg_ref, wu_ref, o_ref):
    a = a_ref[...]
    gate = jnp.dot(a, _mxu_operand(wg_ref), preferred_element_type=F32)
    up = jnp.dot(a, _mxu_operand(wu_ref), preferred_element_type=F32)
    o_ref[...] = (gate * jax.nn.sigmoid(gate) * up).astype(o_ref.dtype)


def gate_up(a, w, tm=2112, tn=256):
    m, k = a.shape
    f = w.shape[1] // 2
    assert m % tm == 0 and f % tn == 0
    nj = f // tn
    return pl.pallas_call(
        _gate_up_kernel,
        grid=(m // tm, nj),
        in_specs=[_resident_rows(tm, k),
                  pl.BlockSpec((k, tn), lambda i, j: (0, j)),
                  pl.BlockSpec((k, tn), lambda i, j: (0, j + nj))],
        out_specs=pl.BlockSpec((tm, tn), lambda i, j: (i, j)),
        out_shape=jax.ShapeDtypeStruct((m, f), BF16),
        compiler_params=_params("arbitrary", "arbitrary"),
        name="gate_up",
    )(a, w, w)


def _hgrn_sum_matrix(c):
    r = np.arange(c)[:, None]
    u = np.arange(c)[None, :]
    blocks = []
    n = c // 2
    while n >= 1:
        m = (r // (2 * n)) * (2 * n) + n - 1
        blocks.append(np.where((r & n) != 0, (u > m) & (u <= r), (u > r) & (u <= m)))
        n //= 2
    blocks.append(u <= r)
    blocks.append(u > r)
    p = np.concatenate(blocks, axis=0).astype(np.float32)
    return np.concatenate([p, p, p, np.zeros_like(p)], axis=1)


def _hgrn_chunks(qz, fz, iz, gz, lb, gn, p, st, c, n_par):
    n_levels = c.bit_length() - 1
    q = qz * jax.nn.sigmoid(qz)
    forget = lb + (1.0 - lb) * jax.nn.sigmoid(fz)
    log_f = jnp.log(forget)
    k = 1.0 - forget
    v16 = iz.astype(BF16)
    gate = gz * jax.nn.sigmoid(gz)

    hi = log_f.astype(BF16)
    rem = log_f - hi.astype(F32)
    mid = rem.astype(BF16)
    lo = (rem - mid.astype(F32)).astype(BF16)
    zero = jnp.zeros((c, HEAD), BF16)
    pieces = [jnp.concatenate([hi[x * c:(x + 1) * c], mid[x * c:(x + 1) * c], lo[x * c:(x + 1) * c], zero],
                              axis=0) for x in range(n_par)]
    rhs = pieces[0] if n_par == 1 else jnp.concatenate(pieces, axis=1)
    decay = jnp.exp(jnp.dot(p, rhs, preferred_element_type=F32))

    t_idx = lax.broadcasted_iota(jnp.int32, (c, c), 0)
    s_idx = lax.broadcasted_iota(jnp.int32, (c, c), 1)
    x_idx = t_idx ^ s_idx
    outs = []
    for x in range(n_par):
        rows = slice(x * c, (x + 1) * c)
        lanes = slice(x * HEAD, (x + 1) * HEAD)
        qx, kx, vx = q[rows], k[rows], v16[rows]
        scores = jnp.zeros((c, c), F32)
        for lvl in reversed(range(n_levels)):
            e = decay[lvl * c:(lvl + 1) * c, lanes]
            s_l = lax.dot_general((qx * e).astype(BF16), (kx * e).astype(BF16), _NT,
                                  preferred_element_type=F32)
            scores = jnp.where(x_idx >= (c >> (lvl + 1)), s_l, scores)
        diag = jnp.where(t_idx == s_idx, jnp.sum(qx * kx, axis=-1, keepdims=True), 0.0)
        scores = jnp.where(t_idx > s_idx, scores, diag)

        e_in = decay[n_levels * c:(n_levels + 1) * c, lanes]
        e_out = decay[(n_levels + 1) * c:, lanes]
        o = jnp.dot(scores.astype(BF16), vx, preferred_element_type=F32)
        o = o + lax.dot_general((qx * e_in).astype(BF16), st.astype(BF16), _NT, preferred_element_type=F32)
        st = st * e_in[c - 1:c, :] + lax.dot_general(vx, (kx * e_out).astype(BF16), _TN,
                                                     preferred_element_type=F32)
        o = o * lax.rsqrt(jnp.mean(o * o, axis=-1, keepdims=True) + EPS) * gn
        outs.append(o * gate[rows])
    return (outs[0] if n_par == 1 else jnp.concatenate(outs, axis=0)), st


def _hgrn_kernel(q_ref, f_ref, i_ref, g_ref, lbl_ref, gn_ref, p_ref, s0_ref, o_ref, s_ref, st_ref,
                 *, layer, chunk, n_par, groups_per_iter, n_iters):
    step = pl.program_id(2)

    @pl.when(step == 0)
    def _():
        st_ref[...] = s0_ref[0, 0].T

    logits = lbl_ref[...]
    ex = jnp.exp(logits - jnp.max(logits, axis=0, keepdims=True))
    sm = ex / jnp.sum(ex, axis=0, keepdims=True)
    lb = jnp.zeros_like(sm[0:1])
    for r in range(1, layer + 1):
        lb = lb + sm[r:r + 1]
    gn = gn_ref[...]
    p = p_ref[...]
    group = n_par * chunk

    def body(it, carry):
        st = st_ref[...]
        for g in range(groups_per_iter):
            rows = pl.ds(pl.multiple_of((it * groups_per_iter + g) * group, group), group)
            o, st = _hgrn_chunks(q_ref[rows, :], f_ref[rows, :], i_ref[rows, :], g_ref[rows, :],
                                 lb, gn, p, st, chunk, n_par)
            o_ref[rows, :] = o.astype(o_ref.dtype)
        st_ref[...] = st
        return carry

    lax.fori_loop(0, n_iters, body, 0)

    @pl.when(step == pl.num_programs(2) - 1)
    def _():
        s_ref[0, 0] = st_ref[...].T.astype(s_ref.dtype)


def hgrn_mixer(z, lb_logits, gnorm, s0, layer, row0, seq, chunk, rows_per_step, n_par, groups_per_iter):
    d = z.shape[1] // 4
    heads = d // HEAD
    batch = s0.shape[0]
    steps = seq // rows_per_step
    blk0 = row0 // rows_per_step
    rows_per_iter = chunk * n_par * groups_per_iter
    assert row0 % rows_per_step == 0 and seq % rows_per_step == 0 and rows_per_step % rows_per_iter == 0
    p = jnp.asarray(_hgrn_sum_matrix(chunk), BF16)

    def col(part):
        return pl.BlockSpec((rows_per_step, HEAD), lambda b, h, s: (blk0 + b * steps + s, part * heads + h))

    head_row = lambda b, h, s: (0, h)
    state_blk = pl.BlockSpec((1, 1, HEAD, HEAD), lambda b, h, s: (b, h, 0, 0))
    return pl.pallas_call(
        functools.partial(_hgrn_kernel, layer=layer, chunk=chunk, n_par=n_par, groups_per_iter=groups_per_iter,
                          n_iters=rows_per_step // rows_per_iter),
        grid=(batch, heads, steps),
        in_specs=[col(0), col(1), col(2), col(3),
                  pl.BlockSpec((lb_logits.shape[0], HEAD), head_row),
                  pl.BlockSpec((1, HEAD), head_row),
                  pl.BlockSpec(p.shape, lambda b, h, s: (0, 0)),
                  state_blk],
        out_specs=[pl.BlockSpec((rows_per_step, HEAD), lambda b, h, s: (b * steps + s, h)), state_blk],
        out_shape=[jax.ShapeDtypeStruct((batch * seq, d), BF16),
                   jax.ShapeDtypeStruct(s0.shape, F32)],
        scratch_shapes=[pltpu.VMEM((HEAD, HEAD), F32)],
        compiler_params=_params("arbitrary", "arbitrary", "arbitrary"),
        name="hgrn_mixer",
    )(z, z, z, z, lb_logits, gnorm.reshape(1, d), p, s0)


def _later_keys_matrix(tk):
    j = lax.broadcasted_iota(jnp.int32, (tk, tk), 0)
    s = lax.broadcasted_iota(jnp.int32, (tk, tk), 1)
    return jnp.where(j > s, 1.0, 0.0).astype(BF16)


def _sb_tile(q, ks, vs, later, acc, run, key_offset):
    z = lax.dot_general(q, ks, _NT, preferred_element_type=F32)
    log_beta = jnp.minimum(z, 0.0) - jnp.log2(1.0 + jnp.exp2(-jnp.abs(z)))
    log_keep = log_beta - z
    if key_offset is not None:
        mask = (lax.broadcasted_iota(jnp.int32, z.shape, 1) + key_offset
                < lax.broadcasted_iota(jnp.int32, z.shape, 0))
        log_keep = jnp.where(mask, log_keep, 0.0)
    after = jnp.dot(log_keep.astype(BF16), later, preferred_element_type=F32)
    w = jnp.exp2(log_beta + after + run)
    if key_offset is not None:
        w = jnp.where(mask, w, 0.0)
    acc = acc + jnp.dot(w.astype(BF16), vs, preferred_element_type=F32)
    run = run + jnp.sum(log_keep, axis=-1, keepdims=True)
    return acc, run


def _sb_prompt_kernel(q_ref, k_ref, v_ref, o_ref, *, tq, tk):
    qi = pl.program_id(2)
    q = q_ref[...]
    later = _later_keys_matrix(tk)
    per_q = tq // tk

    def visit(j, carry, key_offset):
        rows = pl.ds(pl.multiple_of(j * tk, tk), tk)
        return _sb_tile(q, k_ref[rows, :], v_ref[rows, :], later, carry[0], carry[1], key_offset)

    carry = (jnp.zeros((tq, HEAD), F32), jnp.zeros((tq, 1), F32))
    for dgl in reversed(range(per_q)):
        carry = visit(qi * per_q + dgl, carry, dgl * tk)

    def pair(t, c):
        j = qi * per_q - 1 - 2 * t
        return visit(j - 1, visit(j, c, None), None)

    carry = lax.fori_loop(0, qi * (per_q // 2), pair, carry)
    o_ref[...] = carry[0].astype(o_ref.dtype)


def sb_prompt(q, kv16, batch, seq, tq=SB_QUERY_TILE, tk=SB_KEY_TILE):
    d = q.shape[1]
    heads = d // HEAD
    nq = seq // tq
    assert seq % tq == 0 and tq % (2 * tk) == 0
    return pl.pallas_call(
        functools.partial(_sb_prompt_kernel, tq=tq, tk=tk),
        grid=(batch, heads, nq),
        in_specs=[pl.BlockSpec((tq, HEAD), lambda b, h, i: (b * nq + i, h)),
                  pl.BlockSpec((seq, HEAD), lambda b, h, i: (b, h)),
                  pl.BlockSpec((seq, HEAD), lambda b, h, i: (b, heads + h))],
        out_specs=pl.BlockSpec((tq, HEAD), lambda b, h, i: (b * nq + i, h)),
        out_shape=jax.ShapeDtypeStruct((batch * seq, d), BF16),
        compiler_params=_params("arbitrary", "arbitrary", "arbitrary"),
        name="sb_prompt",
    )(q, kv16, kv16)


def _sb_sample_kernel(q_ref, k_ref, v_ref, ck_ref, cv_ref, o_ref, *, tk):
    q = q_ref[...]
    t = q.shape[0]
    past = ck_ref.shape[0]
    carry = (jnp.zeros((t, HEAD), F32), jnp.zeros((t, 1), F32))
    carry = _sb_tile(q, k_ref[...], v_ref[...], _later_keys_matrix(t), carry[0], carry[1], 0)
    later = _later_keys_matrix(tk)
    for j in range(past // tk - 1, -1, -1):
        rows = slice(j * tk, (j + 1) * tk)
        carry = _sb_tile(q, ck_ref[rows, :].astype(BF16), cv_ref[rows, :].astype(BF16), later,
                         carry[0], carry[1], None)
    o_ref[...] = carry[0].astype(o_ref.dtype)


def sb_sample(q, kv16, cache_k, cache_v, row0, batch, seq, tk=SB_KEY_TILE):
    d = q.shape[1]
    heads = d // HEAD
    past = cache_k.shape[1]
    blk0 = row0 // seq
    assert row0 % seq == 0 and past % tk == 0
    cache_blk = pl.BlockSpec((None, past, HEAD), lambda b, h: (b, 0, h))
    return pl.pallas_call(
        functools.partial(_sb_sample_kernel, tk=tk),
        grid=(batch, heads),
        in_specs=[pl.BlockSpec((seq, HEAD), lambda b, h: (blk0 + b, h)),
                  pl.BlockSpec((seq, HEAD), lambda b, h: (blk0 + b, h)),
                  pl.BlockSpec((seq, HEAD), lambda b, h: (blk0 + b, heads + h)),
                  cache_blk, cache_blk],
        out_specs=pl.BlockSpec((seq, HEAD), lambda b, h: (b, h)),
        out_shape=jax.ShapeDtypeStruct((batch * seq, d), BF16),
        compiler_params=_params("arbitrary", "arbitrary"),
        name="sb_sample",
    )(q, kv16, kv16, cache_k.reshape(batch, past, d), cache_v.reshape(batch, past, d))


def kernel(x_prompt, x_sample, state_hgrn, cache_k, cache_v, norm_mix, norm_ffn, norm_kv, norm_out,
           a_w_in, a_lb_logits, a_gnorm, a_w_out, b_w_kv, b_w_q, b_w_out, ffn_w_gate_up, ffn_w_down):
    pb, ps, d = x_prompt.shape
    sb, ss, _ = x_sample.shape
    n_a = a_w_in.shape[0]
    depth = norm_mix.shape[0]
    heads = d // HEAD
    mp, ms = pb * ps, sb * ss
    norm_rows = 384

    x = jnp.concatenate([x_prompt.reshape(mp, d), x_sample.reshape(ms, d)], axis=0)
    zero_state = jnp.zeros((pb, heads, HEAD, HEAD), F32)
    states_p, states_s = [], []
    kv32 = kv16 = None
    for layer in range(depth):
        h = rmsnorm(x, norm_mix[layer], BF16, norm_rows)
        if layer < n_a:
            z = matmul(h, a_w_in[layer])
            o_p, s_p = hgrn_mixer(z, a_lb_logits, a_gnorm[layer], zero_state, layer, 0, ps,
                                  HGRN_CHUNK, 8 * HGRN_CHUNK, 2, 4)
            o_s, s_s = hgrn_mixer(z, a_lb_logits, a_gnorm[layer], state_hgrn[layer], layer, mp, ss,
                                  ss, ss, 1, 1)
            states_p.append(s_p)
            states_s.append(s_s)
            x = matmul(jnp.concatenate([o_p, o_s], axis=0), a_w_out[layer], res=x)
        else:
            j = layer - n_a
            if j == 0:
                kv32, kv16 = matmul(rmsnorm(x, norm_kv, BF16, norm_rows), b_w_kv, out_dtypes=(F32, BF16))
            q = matmul(h, b_w_q[j], out_dtypes=(BF16,), out_scale=SB_LOG2_SCALE)
            o_p = sb_prompt(q, kv16, pb, ps)
            o_s = sb_sample(q, kv16, cache_k, cache_v, mp, sb, ss)
            x = matmul(jnp.concatenate([o_p, o_s], axis=0), b_w_out[j], res=x)
        hidden = gate_up(rmsnorm(x, norm_ffn[layer], BF16, norm_rows), ffn_w_gate_up[layer])
        x = matmul(hidden, ffn_w_down[layer].astype(BF16), res=x, tm=1056, tn=256)
    y_p = rmsnorm(x, norm_out, F32, 256, 0, mp)
    y_s = rmsnorm(x, norm_out, F32, 256, mp, ms)

    k_new, v_new = kv32[:, :d], kv32[:, d:]
    return (y_p.reshape(pb, ps, d), y_s.reshape(sb, ss, d),
            jnp.stack(states_p).astype(x.dtype), k_new[:mp].reshape(pb, ps, heads, HEAD),
            v_new[:mp].reshape(pb, ps, heads, HEAD),
            jnp.stack(states_s).astype(x.dtype), k_new[mp:].reshape(sb, ss, heads, HEAD),
            v_new[mp:].reshape(sb, ss, heads, HEAD))
```

```python
import functools

import numpy as np
import jax
import jax.numpy as jnp
from jax import lax
from jax.experimental import pallas as pl
from jax.experimental.pallas import tpu as pltpu

F32 = jnp.float32
BF16 = jnp.bfloat16
EPS = 1e-6
HEAD = 128
HGRN_CHUNK = 64
SB_KEY_TILE = 256
SB_QUERY_TILE = 2 * SB_KEY_TILE
SB_LOG2_SCALE = float(HEAD ** -0.5 * np.log2(np.e))
SAMPLE_HEADS_PER_STEP = 8
V7X_VMEM_LIMIT_BYTES = 56 * 1024 * 1024

_NT = (((1,), (1,)), ((), ()))
_TN = (((0,), (0,)), ((), ()))


def _params(*sem):
    return pltpu.CompilerParams(dimension_semantics=sem, vmem_limit_bytes=V7X_VMEM_LIMIT_BYTES)


def _rmsnorm_kernel(x_ref, g_ref, o_ref):
    x = x_ref[...]
    y = x * lax.rsqrt(jnp.mean(x * x, axis=-1, keepdims=True) + EPS)
    o_ref[...] = (y * g_ref[...]).astype(o_ref.dtype)


def rmsnorm(x, g, out_dtype, rows, row0=0, nrows=None):
    d = x.shape[1]
    nrows = x.shape[0] if nrows is None else nrows
    assert nrows % rows == 0 and row0 % rows == 0
    blk0 = row0 // rows
    return pl.pallas_call(
        _rmsnorm_kernel,
        grid=(nrows // rows,),
        in_specs=[pl.BlockSpec((rows, d), lambda i: (blk0 + i, 0)), pl.BlockSpec((1, d), lambda i: (0, 0))],
        out_specs=pl.BlockSpec((rows, d), lambda i: (i, 0)),
        out_shape=jax.ShapeDtypeStruct((nrows, d), out_dtype),
        compiler_params=_params("arbitrary"),
        name="rmsnorm",
    )(x, g.reshape(1, d))


def _mxu_operand(w_ref):
    w = w_ref[...]
    return w if w.dtype == BF16 else w.astype(BF16)


def _matmul_kernel(*refs, has_res, n_out, out_scale):
    a_ref, w_ref = refs[0], refs[1]
    acc = jnp.dot(a_ref[...], _mxu_operand(w_ref), preferred_element_type=F32)
    if out_scale is not None:
        acc = acc * out_scale
    if has_res:
        acc = refs[2][...] + acc
    for o_ref in refs[len(refs) - n_out:]:
        o_ref[...] = acc.astype(o_ref.dtype)


def _resident_rows(tm, k):
    return pl.BlockSpec((tm, k), lambda i, j: (i, 0), pipeline_mode=pl.Buffered(1))


def _weight_cols(w, layer, tn, col0=0):
    if w.ndim == 2:
        return pl.BlockSpec((w.shape[0], tn), lambda i, j: (0, j + col0))
    return pl.BlockSpec((None, w.shape[1], tn), lambda i, j: (layer, 0, j + col0))


def matmul(a, w, layer=None, res=None, out_dtypes=(F32,), out_scale=None, tm=2112, tn=256):
    m, k = a.shape
    n = w.shape[-1]
    assert m % tm == 0 and n % tn == 0 and w.shape[-2] == k
    in_specs = [_resident_rows(tm, k), _weight_cols(w, layer, tn)]
    args = [a, w]
    if res is not None:
        in_specs.append(pl.BlockSpec((tm, tn), lambda i, j: (i, j)))
        args.append(res)
    outs = pl.pallas_call(
        functools.partial(_matmul_kernel, has_res=res is not None, n_out=len(out_dtypes), out_scale=out_scale),
        grid=(m // tm, n // tn),
        in_specs=in_specs,
        out_specs=[pl.BlockSpec((tm, tn), lambda i, j: (i, j)) for _ in out_dtypes],
        out_shape=[jax.ShapeDtypeStruct((m, n), dt) for dt in out_dtypes],
        compiler_params=_params("arbitrary", "arbitrary"),
        name="matmul",
    )(*args)
    return outs[0] if len(out_dtypes) == 1 else outs


def _gate_up_kernel(a_ref, wg_ref, wu_ref, o_ref):
    a = a_ref[...]
    gate = jnp.dot(a, _mxu_operand(wg_ref), preferred_element_type=F32)
    up = jnp.dot(a, _mxu_operand(wu_ref), preferred_element_type=F32)
    o_ref[...] = (gate * jax.nn.sigmoid(gate) * up).astype(o_ref.dtype)


def gate_up(a, w, layer=None, tm=2112, tn=256):
    m, k = a.shape
    f = w.shape[-1] // 2
    assert m % tm == 0 and f % tn == 0 and w.shape[-2] == k
    nj = f // tn
    return pl.pallas_call(
        _gate_up_kernel,
        grid=(m // tm, nj),
        in_specs=[_resident_rows(tm, k), _weight_cols(w, layer, tn), _weight_cols(w, layer, tn, nj)],
        out_specs=pl.BlockSpec((tm, tn), lambda i, j: (i, j)),
        out_shape=jax.ShapeDtypeStruct((m, f), BF16),
        compiler_params=_params("arbitrary", "arbitrary"),
        name="gate_up",
    )(a, w, w)


def _mixer_out(dest, rows, d):
    shape = jax.ShapeDtypeStruct((rows, d), BF16)
    if dest is None:
        return shape, [], [], {}
    assert dest.shape == shape.shape and dest.dtype == shape.dtype
    return shape, [pl.BlockSpec(memory_space=pl.ANY)], [dest], None


def _hgrn_sum_matrix(c):
    r = np.arange(c)[:, None]
    u = np.arange(c)[None, :]
    blocks = []
    n = c // 2
    while n >= 1:
        m = (r // (2 * n)) * (2 * n) + n - 1
        blocks.append(np.where((r & n) != 0, (u > m) & (u <= r), (u > r) & (u <= m)))
        n //= 2
    blocks.append(u <= r)
    blocks.append(u > r)
    p = np.concatenate(blocks, axis=0).astype(np.float32)
    return np.concatenate([p, p, p, np.zeros_like(p)], axis=1)


def _hgrn_chunks(qz, fz, iz, gz, lb, gn, p, st, c, n_par):
    n_levels = c.bit_length() - 1
    q = qz * jax.nn.sigmoid(qz)
    forget = lb + (1.0 - lb) * jax.nn.sigmoid(fz)
    log_f = jnp.log(forget)
    k = 1.0 - forget
    v16 = iz.astype(BF16)
    gate = gz * jax.nn.sigmoid(gz)

    hi = log_f.astype(BF16)
    rem = log_f - hi.astype(F32)
    mid = rem.astype(BF16)
    lo = (rem - mid.astype(F32)).astype(BF16)
    zero = jnp.zeros((c, HEAD), BF16)
    pieces = [jnp.concatenate([hi[x * c:(x + 1) * c], mid[x * c:(x + 1) * c], lo[x * c:(x + 1) * c], zero],
                              axis=0) for x in range(n_par)]
    rhs = pieces[0] if n_par == 1 else jnp.concatenate(pieces, axis=1)
    decay = jnp.exp(jnp.dot(p, rhs, preferred_element_type=F32))

    t_idx = lax.broadcasted_iota(jnp.int32, (c, c), 0)
    s_idx = lax.broadcasted_iota(jnp.int32, (c, c), 1)
    x_idx = t_idx ^ s_idx
    outs = []
    for x in range(n_par):
        rows = slice(x * c, (x + 1) * c)
        lanes = slice(x * HEAD, (x + 1) * HEAD)
        qx, kx, vx = q[rows], k[rows], v16[rows]
        scores = jnp.zeros((c, c), F32)
        for lvl in reversed(range(n_levels)):
            e = decay[lvl * c:(lvl + 1) * c, lanes]
            s_l = lax.dot_general((qx * e).astype(BF16), (kx * e).astype(BF16), _NT,
                                  preferred_element_type=F32)
            scores = jnp.where(x_idx >= (c >> (lvl + 1)), s_l, scores)
        diag = jnp.where(t_idx == s_idx, jnp.sum(qx * kx, axis=-1, keepdims=True), 0.0)
        scores = jnp.where(t_idx > s_idx, scores, diag)

        e_in = decay[n_levels * c:(n_levels + 1) * c, lanes]
        e_out = decay[(n_levels + 1) * c:, lanes]
        o = jnp.dot(scores.astype(BF16), vx, preferred_element_type=F32)
        o = o + lax.dot_general((qx * e_in).astype(BF16), st.astype(BF16), _NT, preferred_element_type=F32)
        st = st * e_in[c - 1:c, :] + lax.dot_general(vx, (kx * e_out).astype(BF16), _TN,
                                                     preferred_element_type=F32)
        o = o * lax.rsqrt(jnp.mean(o * o, axis=-1, keepdims=True) + EPS) * gn
        outs.append(o * gate[rows])
    return (outs[0] if n_par == 1 else jnp.concatenate(outs, axis=0)), st


def _hgrn_kernel(*refs, layer, chunk, n_par, groups_per_iter, n_iters, heads_per_step):
    q_ref, f_ref, i_ref, g_ref, lbl_ref, gn_ref, p_ref, s0_ref = refs[:8]
    o_ref, s_ref, st_ref = refs[-3:]
    step = pl.program_id(2)

    @pl.when(step == 0)
    def _():
        for hh in range(heads_per_step):
            st_ref[hh] = s0_ref[0, hh].T

    logits = lbl_ref[...]
    ex = jnp.exp(logits - jnp.max(logits, axis=0, keepdims=True))
    sm = ex / jnp.sum(ex, axis=0, keepdims=True)
    lb = jnp.zeros_like(sm[0:1])
    for r in range(1, layer + 1):
        lb = lb + sm[r:r + 1]
    gn = gn_ref[...]
    p = p_ref[...]
    group = n_par * chunk

    def body(it, carry):
        for hh in range(heads_per_step):
            lanes = slice(hh * HEAD, (hh + 1) * HEAD)
            st = st_ref[hh]
            for g in range(groups_per_iter):
                start = (it * groups_per_iter + g) * group
                rows = pl.ds(start if isinstance(start, int) else pl.multiple_of(start, group), group)
                o, st = _hgrn_chunks(q_ref[rows, lanes], f_ref[rows, lanes], i_ref[rows, lanes],
                                     g_ref[rows, lanes], lb[:, lanes], gn[:, lanes], p, st, chunk, n_par)
                o_ref[rows, lanes] = o.astype(o_ref.dtype)
            st_ref[hh] = st
        return carry

    if n_iters == 1:
        body(0, 0)
    else:
        lax.fori_loop(0, n_iters, body, 0)

    @pl.when(step == pl.num_programs(2) - 1)
    def _():
        for hh in range(heads_per_step):
            s_ref[0, hh] = st_ref[hh].T.astype(s_ref.dtype)


def hgrn_mixer(z, lb_logits, gnorm, s0, layer, row0, seq, chunk, rows_per_step, n_par, groups_per_iter,
               heads_per_step, out_rows, dest=None):
    d = z.shape[1] // 4
    width = heads_per_step * HEAD
    groups = d // width
    batch = s0.shape[0]
    steps = seq // rows_per_step
    blk0 = row0 // rows_per_step
    rows_per_iter = chunk * n_par * groups_per_iter
    assert row0 % rows_per_step == 0 and seq % rows_per_step == 0 and rows_per_step % rows_per_iter == 0
    p = jnp.asarray(_hgrn_sum_matrix(chunk), BF16)

    def col(part):
        return pl.BlockSpec((rows_per_step, width), lambda b, h, s: (blk0 + b * steps + s, part * groups + h))

    head_row = lambda b, h, s: (0, h)
    state_blk = pl.BlockSpec((1, heads_per_step, HEAD, HEAD), lambda b, h, s: (b, h, 0, 0))
    out_shape, dest_specs, dest_args, aliases = _mixer_out(dest, out_rows, d)
    return pl.pallas_call(
        functools.partial(_hgrn_kernel, layer=layer, chunk=chunk, n_par=n_par, groups_per_iter=groups_per_iter,
                          n_iters=rows_per_step // rows_per_iter, heads_per_step=heads_per_step),
        grid=(batch, groups, steps),
        in_specs=[col(0), col(1), col(2), col(3),
                  pl.BlockSpec((lb_logits.shape[0], width), head_row),
                  pl.BlockSpec((1, width), head_row),
                  pl.BlockSpec(p.shape, lambda b, h, s: (0, 0)),
                  state_blk] + dest_specs,
        out_specs=[pl.BlockSpec((rows_per_step, width), lambda b, h, s: (blk0 + b * steps + s, h)), state_blk],
        out_shape=[out_shape, jax.ShapeDtypeStruct(s0.shape, F32)],
        input_output_aliases={8: 0} if aliases is None else aliases,
        scratch_shapes=[pltpu.VMEM((heads_per_step, HEAD, HEAD), F32)],
        compiler_params=_params("arbitrary", "arbitrary", "arbitrary"),
        name="hgrn_mixer",
    )(z, z, z, z, lb_logits, gnorm.reshape(1, d), p, s0, *dest_args)


def _later_keys_matrix(tk):
    j = lax.broadcasted_iota(jnp.int32, (tk, tk), 0)
    s = lax.broadcasted_iota(jnp.int32, (tk, tk), 1)
    return jnp.where(j > s, 1.0, 0.0).astype(BF16)


def _sb_tile(q, ks, vs, later, acc, run, key_offset):
    z = lax.dot_general(q, ks, _NT, preferred_element_type=F32)
    log_beta = jnp.minimum(z, 0.0) - jnp.log2(1.0 + jnp.exp2(-jnp.abs(z)))
    log_keep = log_beta - z
    if key_offset is not None:
        mask = (lax.broadcasted_iota(jnp.int32, z.shape, 1) + key_offset
                < lax.broadcasted_iota(jnp.int32, z.shape, 0))
        log_keep = jnp.where(mask, log_keep, 0.0)
    after = jnp.dot(log_keep.astype(BF16), later, preferred_element_type=F32)
    w = jnp.exp2(log_beta + after + run)
    if key_offset is not None:
        w = jnp.where(mask, w, 0.0)
    acc = acc + jnp.dot(w.astype(BF16), vs, preferred_element_type=F32)
    run = run + jnp.sum(log_keep, axis=-1, keepdims=True)
    return acc, run


def _sb_prompt_kernel(q_ref, k_ref, v_ref, o_ref, *, tq, tk):
    qi = pl.program_id(2)
    q = q_ref[...]
    later = _later_keys_matrix(tk)
    per_q = tq // tk

    def visit(j, carry, key_offset):
        rows = pl.ds(j * tk if isinstance(j, int) else pl.multiple_of(j * tk, tk), tk)
        return _sb_tile(q, k_ref[rows, :], v_ref[rows, :], later, carry[0], carry[1], key_offset)

    carry = (jnp.zeros((tq, HEAD), F32), jnp.zeros((tq, 1), F32))
    for dgl in reversed(range(per_q)):
        carry = visit(qi * per_q + dgl, carry, dgl * tk)

    def quad(t, c):
        j = qi * per_q - 1 - 4 * t
        for u in range(4):
            c = visit(j - u, c, None)
        return c

    n_left = qi * per_q
    carry = lax.fori_loop(0, n_left // 4, quad, carry)
    carry = lax.cond(n_left % 4 != 0, lambda c: visit(0, visit(1, c, None), None), lambda c: c, carry)
    o_ref[...] = carry[0].astype(o_ref.dtype)


def sb_prompt(q, kv16, batch, seq, out_rows, tq=SB_QUERY_TILE, tk=SB_KEY_TILE):
    d = q.shape[1]
    heads = d // HEAD
    nq = seq // tq
    assert seq % tq == 0 and tq == 2 * tk
    return pl.pallas_call(
        functools.partial(_sb_prompt_kernel, tq=tq, tk=tk),
        grid=(batch, heads, nq),
        in_specs=[pl.BlockSpec((tq, HEAD), lambda b, h, i: (b * nq + i, h)),
                  pl.BlockSpec((seq, HEAD), lambda b, h, i: (b, h)),
                  pl.BlockSpec((seq, HEAD), lambda b, h, i: (b, heads + h))],
        out_specs=pl.BlockSpec((tq, HEAD), lambda b, h, i: (b * nq + i, h)),
        out_shape=_mixer_out(None, out_rows, d)[0],
        compiler_params=_params("arbitrary", "arbitrary", "arbitrary"),
        name="sb_prompt",
    )(q, kv16, kv16)


def _sb_sample_kernel(q_ref, k_ref, v_ref, ck_ref, cv_ref, dest_ref, o_ref, *, tk, heads_per_step):
    del dest_ref
    t = q_ref.shape[0]
    past = ck_ref.shape[0]
    later_new = _later_keys_matrix(t)
    later = _later_keys_matrix(tk)
    for hh in range(heads_per_step):
        lanes = slice(hh * HEAD, (hh + 1) * HEAD)
        q = q_ref[:, lanes]
        carry = (jnp.zeros((t, HEAD), F32), jnp.zeros((t, 1), F32))
        carry = _sb_tile(q, k_ref[:, lanes], v_ref[:, lanes], later_new, carry[0], carry[1], 0)
        for j in range(past // tk - 1, -1, -1):
            rows = slice(j * tk, (j + 1) * tk)
            carry = _sb_tile(q, ck_ref[rows, lanes].astype(BF16), cv_ref[rows, lanes].astype(BF16), later,
                             carry[0], carry[1], None)
        o_ref[:, lanes] = carry[0].astype(o_ref.dtype)


def sb_sample(q, kv16, cache_k, cache_v, row0, batch, seq, dest, tk=SB_KEY_TILE,
              heads_per_step=SAMPLE_HEADS_PER_STEP):
    d = q.shape[1]
    width = heads_per_step * HEAD
    groups = d // width
    past = cache_k.shape[1]
    blk0 = row0 // seq
    assert row0 % seq == 0 and past % tk == 0
    own = lambda off: pl.BlockSpec((seq, width), lambda b, h: (blk0 + b, off + h))
    cache_blk = pl.BlockSpec((None, past, width), lambda b, h: (b, 0, h))
    out_shape, dest_specs, dest_args, _ = _mixer_out(dest, dest.shape[0], d)
    return pl.pallas_call(
        functools.partial(_sb_sample_kernel, tk=tk, heads_per_step=heads_per_step),
        grid=(batch, groups),
        in_specs=[own(0), own(0), own(groups), cache_blk, cache_blk] + dest_specs,
        out_specs=own(0),
        out_shape=out_shape,
        input_output_aliases={5: 0},
        compiler_params=_params("arbitrary", "arbitrary"),
        name="sb_sample",
    )(q, kv16, kv16, cache_k.reshape(batch, past, d), cache_v.reshape(batch, past, d), *dest_args)


def kernel(x_prompt, x_sample, state_hgrn, cache_k, cache_v, norm_mix, norm_ffn, norm_kv, norm_out,
           a_w_in, a_lb_logits, a_gnorm, a_w_out, b_w_kv, b_w_q, b_w_out, ffn_w_gate_up, ffn_w_down):
    pb, ps, d = x_prompt.shape
    sb, ss, _ = x_sample.shape
    n_a = a_w_in.shape[0]
    depth = norm_mix.shape[0]
    heads = d // HEAD
    mp, ms = pb * ps, sb * ss
    m = mp + ms
    norm_rows = 384

    x = jnp.concatenate([x_prompt.reshape(mp, d), x_sample.reshape(ms, d)], axis=0)
    zero_state = jnp.zeros((pb, heads, HEAD, HEAD), F32)
    w_down16 = ffn_w_down.astype(BF16)
    states_p, states_s = [], []
    kv32 = kv16 = None
    for layer in range(depth):
        h = rmsnorm(x, norm_mix[layer], BF16, norm_rows)
        if layer < n_a:
            z = matmul(h, a_w_in, layer)
            o, s_p = hgrn_mixer(z, a_lb_logits, a_gnorm[layer], zero_state, layer, 0, ps,
                                HGRN_CHUNK, 8 * HGRN_CHUNK, 2, 4, 1, m)
            o, s_s = hgrn_mixer(z, a_lb_logits, a_gnorm[layer], state_hgrn[layer], layer, mp, ss,
                                ss, ss, 1, 1, SAMPLE_HEADS_PER_STEP, m, dest=o)
            states_p.append(s_p)
            states_s.append(s_s)
            x = matmul(o, a_w_out, layer, res=x)
        else:
            j = layer - n_a
            if j == 0:
                kv32, kv16 = matmul(rmsnorm(x, norm_kv, BF16, norm_rows), b_w_kv, out_dtypes=(F32, BF16))
            q = matmul(h, b_w_q, j, out_dtypes=(BF16,), out_scale=SB_LOG2_SCALE)
            o = sb_prompt(q, kv16, pb, ps, m)
            o = sb_sample(q, kv16, cache_k, cache_v, mp, sb, ss, o)
            x = matmul(o, b_w_out, j, res=x)
        hidden = gate_up(rmsnorm(x, norm_ffn[layer], BF16, norm_rows), ffn_w_gate_up, layer)
        x = matmul(hidden, w_down16, layer, res=x, tm=1056, tn=256)
    y_p = rmsnorm(x, norm_out, F32, 256, 0, mp)
    y_s = rmsnorm(x, norm_out, F32, 256, mp, ms)

    k_new, v_new = kv32[:, :d], kv32[:, d:]
    return (y_p.reshape(pb, ps, d), y_s.reshape(sb, ss, d),
            jnp.stack(states_p).astype(x.dtype), k_new[:mp].reshape(pb, ps, heads, HEAD),
            v_new[:mp].reshape(pb, ps, heads, HEAD),
            jnp.stack(states_s).astype(x.dtype), k_new[mp:].reshape(sb, ss, heads, HEAD),
            v_new[mp:].reshape(sb, ss, heads, HEAD))
```

```python
import functools

import numpy as np
import jax
import jax.numpy as jnp
from jax import lax
from jax.experimental import pallas as pl
from jax.experimental.pallas import tpu as pltpu

F32 = jnp.float32
BF16 = jnp.bfloat16
EPS = 1e-6
HEAD = 128
HGRN_CHUNK = 64
SB_KEY_TILE = 256
SB_QUERY_TILE = 2 * SB_KEY_TILE
SB_LOG2_SCALE = float(HEAD ** -0.5 * np.log2(np.e))
SAMPLE_HEADS_PER_STEP = 8
DOWN_K_BLOCKS = 2
V7X_VMEM_LIMIT_BYTES = 58 * 1024 * 1024

_NT = (((1,), (1,)), ((), ()))
_TN = (((0,), (0,)), ((), ()))


def _params(*sem):
    return pltpu.CompilerParams(dimension_semantics=sem, vmem_limit_bytes=V7X_VMEM_LIMIT_BYTES)


def _rmsnorm_kernel(x_ref, g_ref, o_ref):
    x = x_ref[...]
    y = x * lax.rsqrt(jnp.mean(x * x, axis=-1, keepdims=True) + EPS)
    o_ref[...] = (y * g_ref[...]).astype(o_ref.dtype)


def rmsnorm(x, g, out_dtype, rows, row0=0, nrows=None):
    d = x.shape[1]
    nrows = x.shape[0] if nrows is None else nrows
    assert nrows % rows == 0 and row0 % rows == 0
    blk0 = row0 // rows
    return pl.pallas_call(
        _rmsnorm_kernel,
        grid=(nrows // rows,),
        in_specs=[pl.BlockSpec((rows, d), lambda i: (blk0 + i, 0)), pl.BlockSpec((1, d), lambda i: (0, 0))],
        out_specs=pl.BlockSpec((rows, d), lambda i: (i, 0)),
        out_shape=jax.ShapeDtypeStruct((nrows, d), out_dtype),
        compiler_params=_params("arbitrary"),
        name="rmsnorm",
    )(x, g.reshape(1, d))


def _mxu_operand(w_ref):
    w = w_ref[...]
    return w if w.dtype == BF16 else w.astype(BF16)


def _matmul_kernel(*refs, has_res, n_out, out_scale):
    a_ref, w_ref = refs[0], refs[1]
    acc = jnp.dot(a_ref[...], _mxu_operand(w_ref), preferred_element_type=F32)
    if out_scale is not None:
        acc = acc * out_scale
    if has_res:
        acc = refs[2][...] + acc
    for o_ref in refs[len(refs) - n_out:]:
        o_ref[...] = acc.astype(o_ref.dtype)


def _resident_rows(tm, kt, kb=0):
    return pl.BlockSpec((tm, kt), lambda i, j: (i, kb), pipeline_mode=pl.Buffered(1))


def _weight_cols(w, layer, tn, col0=0, kt=None, kb=0):
    kt = w.shape[-2] if kt is None else kt
    if w.ndim == 2:
        return pl.BlockSpec((kt, tn), lambda i, j: (kb, j + col0))
    return pl.BlockSpec((None, kt, tn), lambda i, j: (layer, kb, j + col0))


def matmul(a, w, layer=None, res=None, out_dtypes=(F32,), out_scale=None, tm=2112, tn=256, k_block=None):
    m, k = a.shape
    n = w.shape[-1]
    kb, kt = (0, k) if k_block is None else k_block
    assert m % tm == 0 and n % tn == 0 and w.shape[-2] == k and k % kt == 0
    in_specs = [_resident_rows(tm, kt, kb), _weight_cols(w, layer, tn, kt=kt, kb=kb)]
    args = [a, w]
    if res is not None:
        in_specs.append(pl.BlockSpec((tm, tn), lambda i, j: (i, j)))
        args.append(res)
    outs = pl.pallas_call(
        functools.partial(_matmul_kernel, has_res=res is not None, n_out=len(out_dtypes), out_scale=out_scale),
        grid=(m // tm, n // tn),
        in_specs=in_specs,
        out_specs=[pl.BlockSpec((tm, tn), lambda i, j: (i, j)) for _ in out_dtypes],
        out_shape=[jax.ShapeDtypeStruct((m, n), dt) for dt in out_dtypes],
        compiler_params=_params("arbitrary", "arbitrary"),
        name="matmul",
    )(*args)
    return outs[0] if len(out_dtypes) == 1 else outs


def _gate_up_kernel(a_ref, wg_ref, wu_ref, o_ref):
    a = a_ref[...]
    gate = jnp.dot(a, _mxu_operand(wg_ref), preferred_element_type=F32)
    up = jnp.dot(a, _mxu_operand(wu_ref), preferred_element_type=F32)
    o_ref[...] = (gate * jax.nn.sigmoid(gate) * up).astype(o_ref.dtype)


def gate_up(a, w, layer=None, tm=2112, tn=256):
    m, k = a.shape
    f = w.shape[-1] // 2
    assert m % tm == 0 and f % tn == 0 and w.shape[-2] == k
    nj = f // tn
    return pl.pallas_call(
        _gate_up_kernel,
        grid=(m // tm, nj),
        in_specs=[_resident_rows(tm, k), _weight_cols(w, layer, tn), _weight_cols(w, layer, tn, nj)],
        out_specs=pl.BlockSpec((tm, tn), lambda i, j: (i, j)),
        out_shape=jax.ShapeDtypeStruct((m, f), BF16),
        compiler_params=_params("arbitrary", "arbitrary"),
        name="gate_up",
    )(a, w, w)


def _mixer_out(dest, rows, d):
    shape = jax.ShapeDtypeStruct((rows, d), BF16)
    if dest is None:
        return shape, [], [], {}
    assert dest.shape == shape.shape and dest.dtype == shape.dtype
    return shape, [pl.BlockSpec(memory_space=pl.ANY)], [dest], None


def _hgrn_sum_matrix(c):
    r = np.arange(c)[:, None]
    u = np.arange(c)[None, :]
    blocks = []
    n = c // 2
    while n >= 1:
        m = (r // (2 * n)) * (2 * n) + n - 1
        blocks.append(np.where((r & n) != 0, (u > m) & (u <= r), (u > r) & (u <= m)))
        n //= 2
    blocks.append(u <= r)
    blocks.append(u > r)
    p = np.concatenate(blocks, axis=0).astype(np.float32)
    return np.concatenate([p, p, p, np.zeros_like(p)], axis=1)


def _hgrn_block(qz, fz, iz, gz, lb, gn, p, st, c, n_sub):
    rows_all = n_sub * c
    n_low = c.bit_length() - 1
    q = qz * jax.nn.sigmoid(qz)
    forget = lb + (1.0 - lb) * jax.nn.sigmoid(fz)
    log_f = jnp.log(forget)
    k = 1.0 - forget
    v16 = iz.astype(BF16)
    gate = gz * jax.nn.sigmoid(gz)

    hi = log_f.astype(BF16)
    rem = log_f - hi.astype(F32)
    mid = rem.astype(BF16)
    lo = (rem - mid.astype(F32)).astype(BF16)
    zero = jnp.zeros((c, HEAD), BF16)
    pieces = [jnp.concatenate([hi[x * c:(x + 1) * c], mid[x * c:(x + 1) * c], lo[x * c:(x + 1) * c], zero],
                              axis=0) for x in range(n_sub)]
    decays = []
    for x in range(0, n_sub - 1, 2):
        pair = jnp.exp(jnp.dot(p, jnp.concatenate(pieces[x:x + 2], axis=1), preferred_element_type=F32))
        decays += [pair[:, :HEAD], pair[:, HEAD:]]
    if n_sub % 2:
        decays.append(jnp.exp(jnp.dot(p, pieces[-1], preferred_element_type=F32)))

    def stack(per_sub):
        return per_sub[0] if n_sub == 1 else jnp.concatenate(per_sub, axis=0)

    def prod(factors):
        out = None
        for f in factors:
            out = f if out is None else out * f
        return out

    def scaled(block, factors):
        f = prod(factors)
        return block if f is None else block * f

    cum_in = [dx[n_low * c:(n_low + 1) * c] for dx in decays]
    cum_out = [dx[(n_low + 1) * c:] for dx in decays]
    whole = [e[c - 1:c, :] for e in cum_in]
    level_decay = [stack([dx[lvl * c:(lvl + 1) * c] for dx in decays]) for lvl in reversed(range(n_low))]
    g = 1
    while g < n_sub:
        per_sub = []
        for x in range(n_sub):
            mid_sub = x - x % (2 * g) + g
            if x >= mid_sub:
                per_sub.append(scaled(cum_in[x], whole[mid_sub:x]))
            else:
                per_sub.append(scaled(cum_out[x], whole[x + 1:mid_sub]))
        level_decay.append(stack(per_sub))
        g *= 2
    e_in = stack([scaled(cum_in[x], whole[:x]) for x in range(n_sub)])
    e_out = stack([scaled(cum_out[x], whole[x + 1:]) for x in range(n_sub)])

    t_idx = lax.broadcasted_iota(jnp.int32, (rows_all, rows_all), 0)
    s_idx = lax.broadcasted_iota(jnp.int32, (rows_all, rows_all), 1)
    x_idx = t_idx ^ s_idx
    scores = jnp.zeros((rows_all, rows_all), F32)
    for lvl, e in enumerate(level_decay):
        s_l = lax.dot_general((q * e).astype(BF16), (k * e).astype(BF16), _NT, preferred_element_type=F32)
        scores = jnp.where(x_idx >= (1 << lvl), s_l, scores)
    diag = jnp.where(t_idx == s_idx, jnp.sum(q * k, axis=-1, keepdims=True), 0.0)
    scores = jnp.where(t_idx > s_idx, scores, diag)

    o = jnp.dot(scores.astype(BF16), v16, preferred_element_type=F32)
    o = o + lax.dot_general((q * e_in).astype(BF16), st.astype(BF16), _NT, preferred_element_type=F32)
    st = st * prod(whole) + lax.dot_general(v16, (k * e_out).astype(BF16), _TN, preferred_element_type=F32)
    o = o * lax.rsqrt(jnp.mean(o * o, axis=-1, keepdims=True) + EPS) * gn
    return o * gate, st


def _hgrn_kernel(*refs, layer, chunk, n_sub, groups_per_iter, n_iters, heads_per_step):
    q_ref, f_ref, i_ref, g_ref, lbl_ref, gn_ref, p_ref, s0_ref = refs[:8]
    o_ref, s_ref, st_ref = refs[-3:]
    step = pl.program_id(2)

    @pl.when(step == 0)
    def _():
        for hh in range(heads_per_step):
            st_ref[hh] = s0_ref[0, hh].T

    logits = lbl_ref[...]
    ex = jnp.exp(logits - jnp.max(logits, axis=0, keepdims=True))
    sm = ex / jnp.sum(ex, axis=0, keepdims=True)
    lb = jnp.zeros_like(sm[0:1])
    for r in range(1, layer + 1):
        lb = lb + sm[r:r + 1]
    gn = gn_ref[...]
    p = p_ref[...]
    group = n_sub * chunk

    def body(it, carry):
        for hh in range(heads_per_step):
            lanes = slice(hh * HEAD, (hh + 1) * HEAD)
            st = st_ref[hh]
            for g in range(groups_per_iter):
                start = (it * groups_per_iter + g) * group
                rows = pl.ds(start if isinstance(start, int) else pl.multiple_of(start, group), group)
                o, st = _hgrn_block(q_ref[rows, lanes], f_ref[rows, lanes], i_ref[rows, lanes],
                                    g_ref[rows, lanes], lb[:, lanes], gn[:, lanes], p, st, chunk, n_sub)
                o_ref[rows, lanes] = o.astype(o_ref.dtype)
            st_ref[hh] = st
        return carry

    if n_iters == 1:
        body(0, 0)
    else:
        lax.fori_loop(0, n_iters, body, 0)

    @pl.when(step == pl.num_programs(2) - 1)
    def _():
        for hh in range(heads_per_step):
            s_ref[0, hh] = st_ref[hh].T.astype(s_ref.dtype)


def hgrn_mixer(z, lb_logits, gnorm, s0, layer, row0, seq, chunk, rows_per_step, n_sub, groups_per_iter,
               heads_per_step, out_rows, dest=None):
    d = z.shape[1] // 4
    width = heads_per_step * HEAD
    groups = d // width
    batch = s0.shape[0]
    steps = seq // rows_per_step
    blk0 = row0 // rows_per_step
    rows_per_iter = chunk * n_sub * groups_per_iter
    assert row0 % rows_per_step == 0 and seq % rows_per_step == 0 and rows_per_step % rows_per_iter == 0
    p = jnp.asarray(_hgrn_sum_matrix(chunk), BF16)

    def col(part):
        return pl.BlockSpec((rows_per_step, width), lambda b, h, s: (blk0 + b * steps + s, part * groups + h))

    head_row = lambda b, h, s: (0, h)
    state_blk = pl.BlockSpec((1, heads_per_step, HEAD, HEAD), lambda b, h, s: (b, h, 0, 0))
    out_shape, dest_specs, dest_args, aliases = _mixer_out(dest, out_rows, d)
    return pl.pallas_call(
        functools.partial(_hgrn_kernel, layer=layer, chunk=chunk, n_sub=n_sub, groups_per_iter=groups_per_iter,
                          n_iters=rows_per_step // rows_per_iter, heads_per_step=heads_per_step),
        grid=(batch, groups, steps),
        in_specs=[col(0), col(1), col(2), col(3),
                  pl.BlockSpec((lb_logits.shape[0], width), head_row),
                  pl.BlockSpec((1, width), head_row),
                  pl.BlockSpec(p.shape, lambda b, h, s: (0, 0)),
                  state_blk] + dest_specs,
        out_specs=[pl.BlockSpec((rows_per_step, width), lambda b, h, s: (blk0 + b * steps + s, h)), state_blk],
        out_shape=[out_shape, jax.ShapeDtypeStruct(s0.shape, F32)],
        input_output_aliases={8: 0} if aliases is None else aliases,
        scratch_shapes=[pltpu.VMEM((heads_per_step, HEAD, HEAD), F32)],
        compiler_params=_params("arbitrary", "arbitrary", "arbitrary"),
        name="hgrn_mixer",
    )(z, z, z, z, lb_logits, gnorm.reshape(1, d), p, s0, *dest_args)


def _later_keys_matrix(tk):
    j = lax.broadcasted_iota(jnp.int32, (tk, tk), 0)
    s = lax.broadcasted_iota(jnp.int32, (tk, tk), 1)
    return jnp.where(j > s, 1.0, 0.0).astype(BF16)


def _sb_tile(q, ks, vs, later, acc, run, key_offset):
    z = lax.dot_general(q, ks, _NT, preferred_element_type=F32)
    log_beta = jnp.minimum(z, 0.0) - jnp.log2(1.0 + jnp.exp2(-jnp.abs(z)))
    log_keep = log_beta - z
    if key_offset is not None:
        mask = (lax.broadcasted_iota(jnp.int32, z.shape, 1) + key_offset
                < lax.broadcasted_iota(jnp.int32, z.shape, 0))
        log_keep = jnp.where(mask, log_keep, 0.0)
    after = jnp.dot(log_keep.astype(BF16), later, preferred_element_type=F32)
    w = jnp.exp2(log_beta + after + run)
    if key_offset is not None:
        w = jnp.where(mask, w, 0.0)
    acc = acc + jnp.dot(w.astype(BF16), vs, preferred_element_type=F32)
    run = run + jnp.sum(log_keep, axis=-1, keepdims=True)
    return acc, run


def _sb_prompt_kernel(q_ref, k_ref, v_ref, o_ref, *, tq, tk):
    qi = pl.program_id(2)
    q = q_ref[...]
    later = _later_keys_matrix(tk)
    per_q = tq // tk

    def visit(j, carry, key_offset):
        rows = pl.ds(j * tk if isinstance(j, int) else pl.multiple_of(j * tk, tk), tk)
        return _sb_tile(q, k_ref[rows, :], v_ref[rows, :], later, carry[0], carry[1], key_offset)

    carry = (jnp.zeros((tq, HEAD), F32), jnp.zeros((tq, 1), F32))
    for dgl in reversed(range(per_q)):
        carry = visit(qi * per_q + dgl, carry, dgl * tk)

    def quad(t, c):
        j = qi * per_q - 1 - 4 * t
        for u in range(4):
            c = visit(j - u, c, None)
        return c

    n_left = qi * per_q
    carry = lax.fori_loop(0, n_left // 4, quad, carry)
    carry = lax.cond(n_left % 4 != 0, lambda c: visit(0, visit(1, c, None), None), lambda c: c, carry)
    o_ref[...] = carry[0].astype(o_ref.dtype)


def sb_prompt(q, kv16, batch, seq, out_rows, tq=SB_QUERY_TILE, tk=SB_KEY_TILE):
    d = q.shape[1]
    heads = d // HEAD
    nq = seq // tq
    assert seq % tq == 0 and tq == 2 * tk
    return pl.pallas_call(
        functools.partial(_sb_prompt_kernel, tq=tq, tk=tk),
        grid=(batch, heads, nq),
        in_specs=[pl.BlockSpec((tq, HEAD), lambda b, h, i: (b * nq + i, h)),
                  pl.BlockSpec((seq, HEAD), lambda b, h, i: (b, h)),
                  pl.BlockSpec((seq, HEAD), lambda b, h, i: (b, heads + h))],
        out_specs=pl.BlockSpec((tq, HEAD), lambda b, h, i: (b * nq + i, h)),
        out_shape=_mixer_out(None, out_rows, d)[0],
        compiler_params=_params("arbitrary", "arbitrary", "arbitrary"),
        name="sb_prompt",
    )(q, kv16, kv16)


def _sb_sample_kernel(q_ref, k_ref, v_ref, ck_ref, cv_ref, dest_ref, o_ref, *, tk, heads_per_step):
    del dest_ref
    t = q_ref.shape[0]
    past = ck_ref.shape[0]
    later_new = _later_keys_matrix(t)
    later = _later_keys_matrix(tk)
    for hh in range(heads_per_step):
        lanes = slice(hh * HEAD, (hh + 1) * HEAD)
        q = q_ref[:, lanes]
        carry = (jnp.zeros((t, HEAD), F32), jnp.zeros((t, 1), F32))
        carry = _sb_tile(q, k_ref[:, lanes], v_ref[:, lanes], later_new, carry[0], carry[1], 0)
        for j in range(past // tk - 1, -1, -1):
            rows = slice(j * tk, (j + 1) * tk)
            carry = _sb_tile(q, ck_ref[rows, lanes].astype(BF16), cv_ref[rows, lanes].astype(BF16), later,
                             carry[0], carry[1], None)
        o_ref[:, lanes] = carry[0].astype(o_ref.dtype)


def sb_sample(q, kv16, cache_k, cache_v, row0, batch, seq, dest, tk=SB_KEY_TILE,
              heads_per_step=SAMPLE_HEADS_PER_STEP):
    d = q.shape[1]
    width = heads_per_step * HEAD
    groups = d // width
    past = cache_k.shape[1]
    blk0 = row0 // seq
    assert row0 % seq == 0 and past % tk == 0
    own = lambda off: pl.BlockSpec((seq, width), lambda b, h: (blk0 + b, off + h))
    cache_blk = pl.BlockSpec((None, past, width), lambda b, h: (b, 0, h))
    out_shape, dest_specs, dest_args, _ = _mixer_out(dest, dest.shape[0], d)
    return pl.pallas_call(
        functools.partial(_sb_sample_kernel, tk=tk, heads_per_step=heads_per_step),
        grid=(batch, groups),
        in_specs=[own(0), own(0), own(groups), cache_blk, cache_blk] + dest_specs,
        out_specs=own(0),
        out_shape=out_shape,
        input_output_aliases={5: 0},
        compiler_params=_params("arbitrary", "arbitrary"),
        name="sb_sample",
    )(q, kv16, kv16, cache_k.reshape(batch, past, d), cache_v.reshape(batch, past, d), *dest_args)


def kernel(x_prompt, x_sample, state_hgrn, cache_k, cache_v, norm_mix, norm_ffn, norm_kv, norm_out,
           a_w_in, a_lb_logits, a_gnorm, a_w_out, b_w_kv, b_w_q, b_w_out, ffn_w_gate_up, ffn_w_down):
    pb, ps, d = x_prompt.shape
    sb, ss, _ = x_sample.shape
    n_a = a_w_in.shape[0]
    depth = norm_mix.shape[0]
    heads = d // HEAD
    mp, ms = pb * ps, sb * ss
    m = mp + ms
    norm_rows = 384

    x = jnp.concatenate([x_prompt.reshape(mp, d), x_sample.reshape(ms, d)], axis=0)
    zero_state = jnp.zeros((pb, heads, HEAD, HEAD), F32)
    states_p, states_s = [], []
    kv32 = kv16 = None
    for layer in range(depth):
        h = rmsnorm(x, norm_mix[layer], BF16, norm_rows)
        if layer < n_a:
            z = matmul(h, a_w_in, layer)
            o, s_p = hgrn_mixer(z, a_lb_logits, a_gnorm[layer], zero_state, layer, 0, ps,
                                HGRN_CHUNK, 16 * HGRN_CHUNK, 4, 4, 1, m)
            o, s_s = hgrn_mixer(z, a_lb_logits, a_gnorm[layer], state_hgrn[layer], layer, mp, ss,
                                ss, ss, 1, 1, SAMPLE_HEADS_PER_STEP, m, dest=o)
            states_p.append(s_p)
            states_s.append(s_s)
            x = matmul(o, a_w_out, layer, res=x)
        else:
            j = layer - n_a
            if j == 0:
                kv32, kv16 = matmul(rmsnorm(x, norm_kv, BF16, norm_rows), b_w_kv, out_dtypes=(F32, BF16))
            q = matmul(h, b_w_q, j, out_dtypes=(BF16,), out_scale=SB_LOG2_SCALE)
            o = sb_prompt(q, kv16, pb, ps, m)
            o = sb_sample(q, kv16, cache_k, cache_v, mp, sb, ss, o)
            x = matmul(o, b_w_out, j, res=x)
        hidden = gate_up(rmsnorm(x, norm_ffn[layer], BF16, norm_rows), ffn_w_gate_up, layer)
        for kb in range(DOWN_K_BLOCKS):
            x = matmul(hidden, ffn_w_down, layer, res=x, k_block=(kb, hidden.shape[1] // DOWN_K_BLOCKS))
    y_p = rmsnorm(x, norm_out, F32, 256, 0, mp)
    y_s = rmsnorm(x, norm_out, F32, 256, mp, ms)

    k_new, v_new = kv32[:, :d], kv32[:, d:]
    return (y_p.reshape(pb, ps, d), y_s.reshape(sb, ss, d),
            jnp.stack(states_p).astype(x.dtype), k_new[:mp].reshape(pb, ps, heads, HEAD),
            v_new[:mp].reshape(pb, ps, heads, HEAD),
            jnp.stack(states_s).astype(x.dtype), k_new[mp:].reshape(sb, ss, heads, HEAD),
            v_new[mp:].reshape(sb, ss, heads, HEAD))
```

```python
import functools

import numpy as np
import jax
import jax.numpy as jnp
from jax import lax
from jax.experimental import pallas as pl
from jax.experimental.pallas import tpu as pltpu

F32 = jnp.float32
BF16 = jnp.bfloat16
EPS = 1e-6
HEAD = 128
HGRN_CHUNK = 64
SB_KEY_TILE = 256
SB_QUERY_TILE = 2 * SB_KEY_TILE
SB_LOG2_SCALE = float(HEAD ** -0.5 * np.log2(np.e))
SAMPLE_HEADS_PER_STEP = 8
LANES = 128
V7X_VMEM_LIMIT_BYTES = 58 * 1024 * 1024

_NT = (((1,), (1,)), ((), ()))
_TN = (((0,), (0,)), ((), ()))


def _params(*sem):
    return pltpu.CompilerParams(dimension_semantics=sem, vmem_limit_bytes=V7X_VMEM_LIMIT_BYTES)


def _rmsnorm_kernel(x_ref, g_ref, o_ref):
    x = x_ref[...]
    y = x * lax.rsqrt(jnp.mean(x * x, axis=-1, keepdims=True) + EPS)
    o_ref[...] = (y * g_ref[...]).astype(o_ref.dtype)


def rmsnorm(x, g, out_dtype, rows, row0=0, nrows=None):
    d = x.shape[1]
    nrows = x.shape[0] if nrows is None else nrows
    assert nrows % rows == 0 and row0 % rows == 0
    blk0 = row0 // rows
    return pl.pallas_call(
        _rmsnorm_kernel,
        grid=(nrows // rows,),
        in_specs=[pl.BlockSpec((rows, d), lambda i: (blk0 + i, 0)), pl.BlockSpec((1, d), lambda i: (0, 0))],
        out_specs=pl.BlockSpec((rows, d), lambda i: (i, 0)),
        out_shape=jax.ShapeDtypeStruct((nrows, d), out_dtype),
        compiler_params=_params("arbitrary"),
        name="rmsnorm",
    )(x, g.reshape(1, d))


def _mxu_operand(w_ref):
    w = w_ref[...]
    return w if w.dtype == BF16 else w.astype(BF16)


def _row_scaled(acc, rnorm_ref):
    return acc * jnp.tile(rnorm_ref[...], (1, acc.shape[1] // LANES))


def _matmul_kernel(*refs, has_rnorm, has_res, n_out, n_gain, out_scale, out_width):
    refs = list(refs)
    a_ref, w_ref = refs.pop(0), refs.pop(0)
    rnorm_ref = refs.pop(0) if has_rnorm else None
    res_ref = refs.pop(0) if has_res else None
    gain_refs = [refs.pop(0) for _ in range(n_gain)]
    out_refs = [refs.pop(0) for _ in range(n_out)]
    normed_refs = [refs.pop(0) for _ in range(n_gain)]
    acc = jnp.dot(a_ref[...], _mxu_operand(w_ref), preferred_element_type=F32)
    if has_rnorm:
        acc = _row_scaled(acc, rnorm_ref)
    if out_scale is not None:
        acc = acc * out_scale
    if has_res:
        acc = res_ref[...] + acc
    for o_ref in out_refs:
        o_ref[...] = acc.astype(o_ref.dtype)
    if n_gain:
        stat_ref = refs.pop(0)
        j = pl.program_id(1)
        for g_ref, o_ref in zip(gain_refs, normed_refs):
            o_ref[...] = (acc * g_ref[...]).astype(o_ref.dtype)
        sq = acc * acc
        part = sq[:, :LANES]
        for c0 in range(LANES, sq.shape[1], LANES):
            part = part + sq[:, c0:c0 + LANES]

        @pl.when(j == 0)
        def _():
            stat_ref[...] = part

        @pl.when(j != 0)
        def _():
            stat_ref[...] = stat_ref[...] + part

        @pl.when(j == pl.num_programs(1) - 1)
        def _():
            mean = jnp.sum(stat_ref[...], axis=-1, keepdims=True) / out_width
            stat_ref[...] = jnp.broadcast_to(lax.rsqrt(mean + EPS), stat_ref.shape)


def _resident_rows(tm, k):
    return pl.BlockSpec((tm, k), lambda i, j: (i, 0), pipeline_mode=pl.Buffered(1))


def _weight_cols(w, layer, tn, col0=0):
    if w.ndim == 2:
        return pl.BlockSpec((w.shape[0], tn), lambda i, j: (0, j + col0))
    return pl.BlockSpec((None, w.shape[1], tn), lambda i, j: (layer, 0, j + col0))


def _row_stat(tm):
    return pl.BlockSpec((tm, LANES), lambda i, j: (i, 0))


def matmul(a, w, layer=None, rnorm=None, res=None, out_dtypes=(F32,), out_scale=None, gains=(), tm=2112, tn=256):
    m, k = a.shape
    n = w.shape[-1]
    assert m % tm == 0 and n % tn == 0 and tn % LANES == 0 and w.shape[-2] == k
    tile = pl.BlockSpec((tm, tn), lambda i, j: (i, j))
    in_specs = [_resident_rows(tm, k), _weight_cols(w, layer, tn)]
    args = [a, w]
    if rnorm is not None:
        in_specs.append(_row_stat(tm))
        args.append(rnorm)
    if res is not None:
        in_specs.append(tile)
        args.append(res)
    for g in gains:
        in_specs.append(pl.BlockSpec((1, tn), lambda i, j: (0, j)))
        args.append(g.reshape(1, n))
    out_specs = [tile for _ in out_dtypes] + [tile for _ in gains]
    out_shape = ([jax.ShapeDtypeStruct((m, n), dt) for dt in out_dtypes]
                 + [jax.ShapeDtypeStruct((m, n), BF16) for _ in gains])
    if gains:
        out_specs.append(_row_stat(tm))
        out_shape.append(jax.ShapeDtypeStruct((m, LANES), F32))
    outs = pl.pallas_call(
        functools.partial(_matmul_kernel, has_rnorm=rnorm is not None, has_res=res is not None,
                          n_out=len(out_dtypes), n_gain=len(gains), out_scale=out_scale, out_width=n),
        grid=(m // tm, n // tn),
        in_specs=in_specs,
        out_specs=out_specs,
        out_shape=out_shape,
        compiler_params=_params("arbitrary", "arbitrary"),
        name="matmul",
    )(*args)
    main = outs[0] if len(out_dtypes) == 1 else outs[:len(out_dtypes)]
    if not gains:
        return main
    return main, outs[len(out_dtypes):-1], outs[-1]


def _gate_up_kernel(*refs, has_rnorm):
    a_ref, wg_ref, wu_ref = refs[:3]
    o_ref = refs[-1]
    a = a_ref[...]
    gate = jnp.dot(a, _mxu_operand(wg_ref), preferred_element_type=F32)
    up = jnp.dot(a, _mxu_operand(wu_ref), preferred_element_type=F32)
    if has_rnorm:
        gate = _row_scaled(gate, refs[3])
        up = _row_scaled(up, refs[3])
    o_ref[...] = (gate * jax.nn.sigmoid(gate) * up).astype(o_ref.dtype)


def gate_up(a, w, layer=None, rnorm=None, tm=2112, tn=256):
    m, k = a.shape
    f = w.shape[-1] // 2
    assert m % tm == 0 and f % tn == 0 and tn % LANES == 0 and w.shape[-2] == k
    nj = f // tn
    in_specs = [_resident_rows(tm, k), _weight_cols(w, layer, tn), _weight_cols(w, layer, tn, nj)]
    args = [a, w, w]
    if rnorm is not None:
        in_specs.append(_row_stat(tm))
        args.append(rnorm)
    return pl.pallas_call(
        functools.partial(_gate_up_kernel, has_rnorm=rnorm is not None),
        grid=(m // tm, nj),
        in_specs=in_specs,
        out_specs=pl.BlockSpec((tm, tn), lambda i, j: (i, j)),
        out_shape=jax.ShapeDtypeStruct((m, f), BF16),
        compiler_params=_params("arbitrary", "arbitrary"),
        name="gate_up",
    )(*args)


def _mixer_out(dest, rows, d):
    shape = jax.ShapeDtypeStruct((rows, d), BF16)
    if dest is None:
        return shape, [], [], {}
    assert dest.shape == shape.shape and dest.dtype == shape.dtype
    return shape, [pl.BlockSpec(memory_space=pl.ANY)], [dest], None


def _hgrn_sum_matrix(c):
    r = np.arange(c)[:, None]
    u = np.arange(c)[None, :]
    blocks = []
    n = c // 2
    while n >= 1:
        m = (r // (2 * n)) * (2 * n) + n - 1
        blocks.append(np.where((r & n) != 0, (u > m) & (u <= r), (u > r) & (u <= m)))
        n //= 2
    blocks.append(u <= r)
    blocks.append(u > r)
    p = np.concatenate(blocks, axis=0).astype(np.float32)
    return np.concatenate([p, p, p, np.zeros_like(p)], axis=1)


def _hgrn_block(qz, fz, iz, gz, lb, gn, p, st, c, n_sub):
    rows_all = n_sub * c
    n_low = c.bit_length() - 1
    q = qz * jax.nn.sigmoid(qz)
    forget = lb + (1.0 - lb) * jax.nn.sigmoid(fz)
    log_f = jnp.log(forget)
    k = 1.0 - forget
    v16 = iz.astype(BF16)
    gate = gz * jax.nn.sigmoid(gz)

    hi = log_f.astype(BF16)
    rem = log_f - hi.astype(F32)
    mid = rem.astype(BF16)
    lo = (rem - mid.astype(F32)).astype(BF16)
    zero = jnp.zeros((c, HEAD), BF16)
    pieces = [jnp.concatenate([hi[x * c:(x + 1) * c], mid[x * c:(x + 1) * c], lo[x * c:(x + 1) * c], zero],
                              axis=0) for x in range(n_sub)]
    decays = []
    for x in range(0, n_sub - 1, 2):
        pair = jnp.exp(jnp.dot(p, jnp.concatenate(pieces[x:x + 2], axis=1), preferred_element_type=F32))
        decays += [pair[:, :HEAD], pair[:, HEAD:]]
    if n_sub % 2:
        decays.append(jnp.exp(jnp.dot(p, pieces[-1], preferred_element_type=F32)))

    def stack(per_sub):
        return per_sub[0] if n_sub == 1 else jnp.concatenate(per_sub, axis=0)

    def prod(factors):
        out = None
        for f in factors:
            out = f if out is None else out * f
        return out

    def scaled(block, factors):
        f = prod(factors)
        return block if f is None else block * f

    cum_in = [dx[n_low * c:(n_low + 1) * c] for dx in decays]
    cum_out = [dx[(n_low + 1) * c:] for dx in decays]
    whole = [e[c - 1:c, :] for e in cum_in]
    level_decay = [stack([dx[lvl * c:(lvl + 1) * c] for dx in decays]) for lvl in reversed(range(n_low))]
    g = 1
    while g < n_sub:
        per_sub = []
        for x in range(n_sub):
            mid_sub = x - x % (2 * g) + g
            if x >= mid_sub:
                per_sub.append(scaled(cum_in[x], whole[mid_sub:x]))
            else:
                per_sub.append(scaled(cum_out[x], whole[x + 1:mid_sub]))
        level_decay.append(stack(per_sub))
        g *= 2
    e_in = stack([scaled(cum_in[x], whole[:x]) for x in range(n_sub)])
    e_out = stack([scaled(cum_out[x], whole[x + 1:]) for x in range(n_sub)])

    t_idx = lax.broadcasted_iota(jnp.int32, (rows_all, rows_all), 0)
    s_idx = lax.broadcasted_iota(jnp.int32, (rows_all, rows_all), 1)
    x_idx = t_idx ^ s_idx
    scores = jnp.zeros((rows_all, rows_all), F32)
    for lvl, e in enumerate(level_decay):
        s_l = lax.dot_general((q * e).astype(BF16), (k * e).astype(BF16), _NT, preferred_element_type=F32)
        scores = jnp.where(x_idx >= (1 << lvl), s_l, scores)
    diag = jnp.where(t_idx == s_idx, jnp.sum(q * k, axis=-1, keepdims=True), 0.0)
    scores = jnp.where(t_idx > s_idx, scores, diag)

    o = jnp.dot(scores.astype(BF16), v16, preferred_element_type=F32)
    o = o + lax.dot_general((q * e_in).astype(BF16), st.astype(BF16), _NT, preferred_element_type=F32)
    st = st * prod(whole) + lax.dot_general(v16, (k * e_out).astype(BF16), _TN, preferred_element_type=F32)
    o = o * lax.rsqrt(jnp.mean(o * o, axis=-1, keepdims=True) + EPS) * gn
    return o * gate, st


def _hgrn_kernel(*refs, layer, chunk, n_sub, groups_per_iter, n_iters, heads_per_step):
    q_ref, f_ref, i_ref, g_ref, lbl_ref, gn_ref, p_ref, s0_ref = refs[:8]
    o_ref, s_ref, st_ref = refs[-3:]
    step = pl.program_id(2)

    @pl.when(step == 0)
    def _():
        for hh in range(heads_per_step):
            st_ref[hh] = s0_ref[0, hh].T

    logits = lbl_ref[...]
    ex = jnp.exp(logits - jnp.max(logits, axis=0, keepdims=True))
    sm = ex / jnp.sum(ex, axis=0, keepdims=True)
    lb = jnp.zeros_like(sm[0:1])
    for r in range(1, layer + 1):
        lb = lb + sm[r:r + 1]
    gn = gn_ref[...]
    p = p_ref[...]
    group = n_sub * chunk

    def body(it, carry):
        for hh in range(heads_per_step):
            lanes = slice(hh * HEAD, (hh + 1) * HEAD)
            st = st_ref[hh]
            for g in range(groups_per_iter):
                start = (it * groups_per_iter + g) * group
                rows = pl.ds(start if isinstance(start, int) else pl.multiple_of(start, group), group)
                o, st = _hgrn_block(q_ref[rows, lanes], f_ref[rows, lanes], i_ref[rows, lanes],
                                    g_ref[rows, lanes], lb[:, lanes], gn[:, lanes], p, st, chunk, n_sub)
                o_ref[rows, lanes] = o.astype(o_ref.dtype)
            st_ref[hh] = st
        return carry

    if n_iters == 1:
        body(0, 0)
    else:
        lax.fori_loop(0, n_iters, body, 0)

    @pl.when(step == pl.num_programs(2) - 1)
    def _():
        for hh in range(heads_per_step):
            s_ref[0, hh] = st_ref[hh].T.astype(s_ref.dtype)


def hgrn_mixer(z, lb_logits, gnorm, s0, layer, row0, seq, chunk, rows_per_step, n_sub, groups_per_iter,
               heads_per_step, out_rows, dest=None):
    d = z.shape[1] // 4
    width = heads_per_step * HEAD
    groups = d // width
    batch = s0.shape[0]
    steps = seq // rows_per_step
    blk0 = row0 // rows_per_step
    rows_per_iter = chunk * n_sub * groups_per_iter
    assert row0 % rows_per_step == 0 and seq % rows_per_step == 0 and rows_per_step % rows_per_iter == 0
    p = jnp.asarray(_hgrn_sum_matrix(chunk), BF16)

    def col(part):
        return pl.BlockSpec((rows_per_step, width), lambda b, h, s: (blk0 + b * steps + s, part * groups + h))

    head_row = lambda b, h, s: (0, h)
    state_blk = pl.BlockSpec((1, heads_per_step, HEAD, HEAD), lambda b, h, s: (b, h, 0, 0))
    out_shape, dest_specs, dest_args, aliases = _mixer_out(dest, out_rows, d)
    return pl.pallas_call(
        functools.partial(_hgrn_kernel, layer=layer, chunk=chunk, n_sub=n_sub, groups_per_iter=groups_per_iter,
                          n_iters=rows_per_step // rows_per_iter, heads_per_step=heads_per_step),
        grid=(batch, groups, steps),
        in_specs=[col(0), col(1), col(2), col(3),
                  pl.BlockSpec((lb_logits.shape[0], width), head_row),
                  pl.BlockSpec((1, width), head_row),
                  pl.BlockSpec(p.shape, lambda b, h, s: (0, 0)),
                  state_blk] + dest_specs,
        out_specs=[pl.BlockSpec((rows_per_step, width), lambda b, h, s: (blk0 + b * steps + s, h)), state_blk],
        out_shape=[out_shape, jax.ShapeDtypeStruct(s0.shape, F32)],
        input_output_aliases={8: 0} if aliases is None else aliases,
        scratch_shapes=[pltpu.VMEM((heads_per_step, HEAD, HEAD), F32)],
        compiler_params=_params("arbitrary", "arbitrary", "arbitrary"),
        name="hgrn_mixer",
    )(z, z, z, z, lb_logits, gnorm.reshape(1, d), p, s0, *dest_args)


def _later_keys_matrix(tk):
    j = lax.broadcasted_iota(jnp.int32, (tk, tk), 0)
    s = lax.broadcasted_iota(jnp.int32, (tk, tk), 1)
    return jnp.where(j > s, 1.0, 0.0).astype(BF16)


def _sb_tile(q, ks, vs, later, acc, run, key_offset):
    z = lax.dot_general(q, ks, _NT, preferred_element_type=F32)
    log_beta = jnp.minimum(z, 0.0) - jnp.log2(1.0 + jnp.exp2(-jnp.abs(z)))
    log_keep = log_beta - z
    if key_offset is not None:
        mask = (lax.broadcasted_iota(jnp.int32, z.shape, 1) + key_offset
                < lax.broadcasted_iota(jnp.int32, z.shape, 0))
        log_keep = jnp.where(mask, log_keep, 0.0)
    after = jnp.dot(log_keep.astype(BF16), later, preferred_element_type=F32)
    w = jnp.exp2(log_beta + after + run)
    if key_offset is not None:
        w = jnp.where(mask, w, 0.0)
    acc = acc + jnp.dot(w.astype(BF16), vs, preferred_element_type=F32)
    run = run + jnp.sum(log_keep, axis=-1, keepdims=True)
    return acc, run


def _sb_prompt_kernel(q_ref, k_ref, v_ref, o_ref, *, tq, tk):
    qi = pl.program_id(2)
    q = q_ref[...]
    later = _later_keys_matrix(tk)
    per_q = tq // tk

    def visit(j, carry, key_offset):
        rows = pl.ds(j * tk if isinstance(j, int) else pl.multiple_of(j * tk, tk), tk)
        return _sb_tile(q, k_ref[rows, :], v_ref[rows, :], later, carry[0], carry[1], key_offset)

    carry = (jnp.zeros((tq, HEAD), F32), jnp.zeros((tq, 1), F32))
    for dgl in reversed(range(per_q)):
        carry = visit(qi * per_q + dgl, carry, dgl * tk)

    def quad(t, c):
        j = qi * per_q - 1 - 4 * t
        for u in range(4):
            c = visit(j - u, c, None)
        return c

    n_left = qi * per_q
    carry = lax.fori_loop(0, n_left // 4, quad, carry)
    carry = lax.cond(n_left % 4 != 0, lambda c: visit(0, visit(1, c, None), None), lambda c: c, carry)
    o_ref[...] = carry[0].astype(o_ref.dtype)


def sb_prompt(q, kv16, batch, seq, out_rows, tq=SB_QUERY_TILE, tk=SB_KEY_TILE):
    d = q.shape[1]
    heads = d // HEAD
    nq = seq // tq
    assert seq % tq == 0 and tq == 2 * tk
    return pl.pallas_call(
        functools.partial(_sb_prompt_kernel, tq=tq, tk=tk),
        grid=(batch, heads, nq),
        in_specs=[pl.BlockSpec((tq, HEAD), lambda b, h, i: (b * nq + i, h)),
                  pl.BlockSpec((seq, HEAD), lambda b, h, i: (b, h)),
                  pl.BlockSpec((seq, HEAD), lambda b, h, i: (b, heads + h))],
        out_specs=pl.BlockSpec((tq, HEAD), lambda b, h, i: (b * nq + i, h)),
        out_shape=_mixer_out(None, out_rows, d)[0],
        compiler_params=_params("arbitrary", "arbitrary", "arbitrary"),
        name="sb_prompt",
    )(q, kv16, kv16)


def _sb_sample_kernel(q_ref, k_ref, v_ref, ck_ref, cv_ref, dest_ref, o_ref, *, tk, heads_per_step):
    del dest_ref
    t = q_ref.shape[0]
    past = ck_ref.shape[0]
    later_new = _later_keys_matrix(t)
    later = _later_keys_matrix(tk)
    for hh in range(heads_per_step):
        lanes = slice(hh * HEAD, (hh + 1) * HEAD)
        q = q_ref[:, lanes]
        carry = (jnp.zeros((t, HEAD), F32), jnp.zeros((t, 1), F32))
        carry = _sb_tile(q, k_ref[:, lanes], v_ref[:, lanes], later_new, carry[0], carry[1], 0)
        for j in range(past // tk - 1, -1, -1):
            rows = slice(j * tk, (j + 1) * tk)
            carry = _sb_tile(q, ck_ref[rows, lanes].astype(BF16), cv_ref[rows, lanes].astype(BF16), later,
                             carry[0], carry[1], None)
        o_ref[:, lanes] = carry[0].astype(o_ref.dtype)


def sb_sample(q, kv16, cache_k, cache_v, row0, batch, seq, dest, tk=SB_KEY_TILE,
              heads_per_step=SAMPLE_HEADS_PER_STEP):
    d = q.shape[1]
    width = heads_per_step * HEAD
    groups = d // width
    past = cache_k.shape[1]
    blk0 = row0 // seq
    assert row0 % seq == 0 and past % tk == 0
    own = lambda off: pl.BlockSpec((seq, width), lambda b, h: (blk0 + b, off + h))
    cache_blk = pl.BlockSpec((None, past, width), lambda b, h: (b, 0, h))
    out_shape, dest_specs, dest_args, _ = _mixer_out(dest, dest.shape[0], d)
    return pl.pallas_call(
        functools.partial(_sb_sample_kernel, tk=tk, heads_per_step=heads_per_step),
        grid=(batch, groups),
        in_specs=[own(0), own(0), own(groups), cache_blk, cache_blk] + dest_specs,
        out_specs=own(0),
        out_shape=out_shape,
        input_output_aliases={5: 0},
        compiler_params=_params("arbitrary", "arbitrary"),
        name="sb_sample",
    )(q, kv16, kv16, cache_k.reshape(batch, past, d), cache_v.reshape(batch, past, d), *dest_args)


def kernel(x_prompt, x_sample, state_hgrn, cache_k, cache_v, norm_mix, norm_ffn, norm_kv, norm_out,
           a_w_in, a_lb_logits, a_gnorm, a_w_out, b_w_kv, b_w_q, b_w_out, ffn_w_gate_up, ffn_w_down):
    pb, ps, d = x_prompt.shape
    sb, ss, _ = x_sample.shape
    n_a = a_w_in.shape[0]
    depth = norm_mix.shape[0]
    heads = d // HEAD
    mp, ms = pb * ps, sb * ss
    m = mp + ms

    x = jnp.concatenate([x_prompt.reshape(mp, d), x_sample.reshape(ms, d)], axis=0)
    zero_state = jnp.zeros((pb, heads, HEAD, HEAD), F32)
    w_down16 = ffn_w_down.astype(BF16)
    states_p, states_s = [], []
    kv32 = kv16 = h_kv = None
    h, rnorm = rmsnorm(x, norm_mix[0], BF16, 384), None
    for layer in range(depth):
        if layer < n_a:
            z = matmul(h, a_w_in, layer, rnorm)
            o, s_p = hgrn_mixer(z, a_lb_logits, a_gnorm[layer], zero_state, layer, 0, ps,
                                HGRN_CHUNK, 64 * HGRN_CHUNK, 4, 16, 1, m)
            o, s_s = hgrn_mixer(z, a_lb_logits, a_gnorm[layer], state_hgrn[layer], layer, mp, ss,
                                ss, ss, 1, 1, SAMPLE_HEADS_PER_STEP, m, dest=o)
            states_p.append(s_p)
            states_s.append(s_s)
            x, (h_ffn,), rnorm_ffn = matmul(o, a_w_out, layer, res=x, gains=(norm_ffn[layer],))
        else:
            j = layer - n_a
            if j == 0:
                kv32, kv16 = matmul(h_kv, b_w_kv, None, rnorm, out_dtypes=(F32, BF16))
            q = matmul(h, b_w_q, j, rnorm, out_dtypes=(BF16,), out_scale=SB_LOG2_SCALE)
            o = sb_prompt(q, kv16, pb, ps, m)
            o = sb_sample(q, kv16, cache_k, cache_v, mp, sb, ss, o)
            x, (h_ffn,), rnorm_ffn = matmul(o, b_w_out, j, res=x, gains=(norm_ffn[layer],))
        hidden = gate_up(h_ffn, ffn_w_gate_up, layer, rnorm_ffn)
        if layer + 1 == depth:
            x = matmul(hidden, w_down16, layer, res=x, tm=1056)
        elif layer + 1 == n_a:
            x, (h, h_kv), rnorm = matmul(hidden, w_down16, layer, res=x, gains=(norm_mix[layer + 1], norm_kv),
                                         tm=1056)
        else:
            x, (h,), rnorm = matmul(hidden, w_down16, layer, res=x, gains=(norm_mix[layer + 1],), tm=1056)
    y_p = rmsnorm(x, norm_out, F32, 256, 0, mp)
    y_s = rmsnorm(x, norm_out, F32, 256, mp, ms)

    k_new, v_new = kv32[:, :d], kv32[:, d:]
    return (y_p.reshape(pb, ps, d), y_s.reshape(sb, ss, d),
            jnp.stack(states_p).astype(x.dtype), k_new[:mp].reshape(pb, ps, heads, HEAD),
            v_new[:mp].reshape(pb, ps, heads, HEAD),
            jnp.stack(states_s).astype(x.dtype), k_new[mp:].reshape(sb, ss, heads, HEAD),
            v_new[mp:].reshape(sb, ss, heads, HEAD))
```

```python
import functools

import numpy as np
import jax
import jax.numpy as jnp
from jax import lax
from jax.experimental import pallas as pl
from jax.experimental.pallas import tpu as pltpu

F32 = jnp.float32
BF16 = jnp.bfloat16
EPS = 1e-6
HEAD = 128
HGRN_CHUNK = 64
SB_KEY_TILE = 256
SB_QUERY_TILE = 2 * SB_KEY_TILE
SB_SCALE = HEAD ** -0.5
SAMPLE_HEADS_PER_STEP = 8
LANES = 128
V7X_VMEM_LIMIT_BYTES = 58 * 1024 * 1024

_NT = (((1,), (1,)), ((), ()))
_TN = (((0,), (0,)), ((), ()))


def _params(*sem):
    return pltpu.CompilerParams(dimension_semantics=sem, vmem_limit_bytes=V7X_VMEM_LIMIT_BYTES)


def _rmsnorm_kernel(x_ref, g_ref, o_ref):
    x = x_ref[...]
    y = x * lax.rsqrt(jnp.mean(x * x, axis=-1, keepdims=True) + EPS)
    o_ref[...] = (y * g_ref[...]).astype(o_ref.dtype)


def rmsnorm(x, g, out_dtype, rows, row0=0, nrows=None):
    d = x.shape[1]
    nrows = x.shape[0] if nrows is None else nrows
    assert nrows % rows == 0 and row0 % rows == 0
    blk0 = row0 // rows
    return pl.pallas_call(
        _rmsnorm_kernel,
        grid=(nrows // rows,),
        in_specs=[pl.BlockSpec((rows, d), lambda i: (blk0 + i, 0)), pl.BlockSpec((1, d), lambda i: (0, 0))],
        out_specs=pl.BlockSpec((rows, d), lambda i: (i, 0)),
        out_shape=jax.ShapeDtypeStruct((nrows, d), out_dtype),
        compiler_params=_params("arbitrary"),
        name="rmsnorm",
    )(x, g.reshape(1, d))


def _mxu_operand(w_ref):
    w = w_ref[...]
    return w if w.dtype == BF16 else w.astype(BF16)


def _row_scaled(acc, rnorm_ref):
    return acc * jnp.tile(rnorm_ref[...], (1, acc.shape[1] // LANES))


def _matmul_kernel(*refs, has_rnorm, has_res, n_out, n_gain, out_scale, out_width):
    refs = list(refs)
    a_ref, w_ref = refs.pop(0), refs.pop(0)
    rnorm_ref = refs.pop(0) if has_rnorm else None
    res_ref = refs.pop(0) if has_res else None
    gain_refs = [refs.pop(0) for _ in range(n_gain)]
    out_refs = [refs.pop(0) for _ in range(n_out)]
    normed_refs = [refs.pop(0) for _ in range(n_gain)]
    acc = jnp.dot(a_ref[...], _mxu_operand(w_ref), preferred_element_type=F32)
    if has_rnorm:
        acc = _row_scaled(acc, rnorm_ref)
    if out_scale is not None:
        acc = acc * out_scale
    if has_res:
        acc = res_ref[...] + acc
    for o_ref in out_refs:
        o_ref[...] = acc.astype(o_ref.dtype)
    if n_gain:
        stat_ref = refs.pop(0)
        j = pl.program_id(1)
        for g_ref, o_ref in zip(gain_refs, normed_refs):
            o_ref[...] = (acc * g_ref[...]).astype(o_ref.dtype)
        sq = acc * acc
        part = sq[:, :LANES]
        for c0 in range(LANES, sq.shape[1], LANES):
            part = part + sq[:, c0:c0 + LANES]

        @pl.when(j == 0)
        def _():
            stat_ref[...] = part

        @pl.when(j != 0)
        def _():
            stat_ref[...] = stat_ref[...] + part

        @pl.when(j == pl.num_programs(1) - 1)
        def _():
            mean = jnp.sum(stat_ref[...], axis=-1, keepdims=True) / out_width
            stat_ref[...] = jnp.broadcast_to(lax.rsqrt(mean + EPS), stat_ref.shape)


def _resident_rows(tm, k):
    return pl.BlockSpec((tm, k), lambda i, j: (i, 0), pipeline_mode=pl.Buffered(1))


def _weight_cols(w, layer, tn, col0=0):
    if w.ndim == 2:
        return pl.BlockSpec((w.shape[0], tn), lambda i, j: (0, j + col0))
    return pl.BlockSpec((None, w.shape[1], tn), lambda i, j: (layer, 0, j + col0))


def _row_stat(tm):
    return pl.BlockSpec((tm, LANES), lambda i, j: (i, 0))


def matmul(a, w, layer=None, rnorm=None, res=None, out_dtypes=(F32,), out_scale=None, gains=(), tm=2112, tn=256):
    m, k = a.shape
    n = w.shape[-1]
    assert m % tm == 0 and n % tn == 0 and tn % LANES == 0 and w.shape[-2] == k
    tile = pl.BlockSpec((tm, tn), lambda i, j: (i, j))
    in_specs = [_resident_rows(tm, k), _weight_cols(w, layer, tn)]
    args = [a, w]
    if rnorm is not None:
        in_specs.append(_row_stat(tm))
        args.append(rnorm)
    if res is not None:
        in_specs.append(tile)
        args.append(res)
    for g in gains:
        in_specs.append(pl.BlockSpec((1, tn), lambda i, j: (0, j)))
        args.append(g.reshape(1, n))
    out_specs = [tile for _ in out_dtypes] + [tile for _ in gains]
    out_shape = ([jax.ShapeDtypeStruct((m, n), dt) for dt in out_dtypes]
                 + [jax.ShapeDtypeStruct((m, n), BF16) for _ in gains])
    if gains:
        out_specs.append(_row_stat(tm))
        out_shape.append(jax.ShapeDtypeStruct((m, LANES), F32))
    outs = pl.pallas_call(
        functools.partial(_matmul_kernel, has_rnorm=rnorm is not None, has_res=res is not None,
                          n_out=len(out_dtypes), n_gain=len(gains), out_scale=out_scale, out_width=n),
        grid=(m // tm, n // tn),
        in_specs=in_specs,
        out_specs=out_specs,
        out_shape=out_shape,
        compiler_params=_params("arbitrary", "arbitrary"),
        name="matmul",
    )(*args)
    main = outs[0] if len(out_dtypes) == 1 else outs[:len(out_dtypes)]
    if not gains:
        return main
    return main, outs[len(out_dtypes):-1], outs[-1]


def _gate_up_kernel(*refs, has_rnorm):
    a_ref, wg_ref, wu_ref = refs[:3]
    o_ref = refs[-1]
    a = a_ref[...]
    gate = jnp.dot(a, _mxu_operand(wg_ref), preferred_element_type=F32)
    up = jnp.dot(a, _mxu_operand(wu_ref), preferred_element_type=F32)
    if has_rnorm:
        gate = _row_scaled(gate, refs[3])
        up = _row_scaled(up, refs[3])
    o_ref[...] = (gate * jax.nn.sigmoid(gate) * up).astype(o_ref.dtype)


def gate_up(a, w, layer=None, rnorm=None, tm=2112, tn=256):
    m, k = a.shape
    f = w.shape[-1] // 2
    assert m % tm == 0 and f % tn == 0 and tn % LANES == 0 and w.shape[-2] == k
    nj = f // tn
    in_specs = [_resident_rows(tm, k), _weight_cols(w, layer, tn), _weight_cols(w, layer, tn, nj)]
    args = [a, w, w]
    if rnorm is not None:
        in_specs.append(_row_stat(tm))
        args.append(rnorm)
    return pl.pallas_call(
        functools.partial(_gate_up_kernel, has_rnorm=rnorm is not None),
        grid=(m // tm, nj),
        in_specs=in_specs,
        out_specs=pl.BlockSpec((tm, tn), lambda i, j: (i, j)),
        out_shape=jax.ShapeDtypeStruct((m, f), BF16),
        compiler_params=_params("arbitrary", "arbitrary"),
        name="gate_up",
    )(*args)


def _mixer_out(dest, d):
    assert dest.shape[1] == d and dest.dtype == BF16
    return jax.ShapeDtypeStruct(dest.shape, BF16), pl.BlockSpec(memory_space=pl.ANY)


def _hgrn_sum_matrix(c):
    r = np.arange(c)[:, None]
    u = np.arange(c)[None, :]
    blocks = []
    n = c // 2
    while n >= 1:
        m = (r // (2 * n)) * (2 * n) + n - 1
        blocks.append(np.where((r & n) != 0, (u > m) & (u <= r), (u > r) & (u <= m)))
        n //= 2
    blocks.append(u <= r)
    blocks.append(u > r)
    p = np.concatenate(blocks, axis=0).astype(np.float32)
    return np.concatenate([p, p, p, np.zeros_like(p)], axis=1)


def _hgrn_block(qz, fz, iz, gz, lb, gn, p, st, c, n_sub):
    rows_all = n_sub * c
    n_low = c.bit_length() - 1
    q = qz * jax.nn.sigmoid(qz)
    forget = lb + (1.0 - lb) * jax.nn.sigmoid(fz)
    log_f = jnp.log(forget)
    k = 1.0 - forget
    v16 = iz.astype(BF16)
    gate = gz * jax.nn.sigmoid(gz)

    hi = log_f.astype(BF16)
    rem = log_f - hi.astype(F32)
    mid = rem.astype(BF16)
    lo = (rem - mid.astype(F32)).astype(BF16)
    zero = jnp.zeros((c, HEAD), BF16)
    pieces = [jnp.concatenate([hi[x * c:(x + 1) * c], mid[x * c:(x + 1) * c], lo[x * c:(x + 1) * c], zero],
                              axis=0) for x in range(n_sub)]
    decays = []
    for x in range(0, n_sub - 1, 2):
        pair = jnp.exp(jnp.dot(p, jnp.concatenate(pieces[x:x + 2], axis=1), preferred_element_type=F32))
        decays += [pair[:, :HEAD], pair[:, HEAD:]]
    if n_sub % 2:
        decays.append(jnp.exp(jnp.dot(p, pieces[-1], preferred_element_type=F32)))

    def stack(per_sub):
        return per_sub[0] if n_sub == 1 else jnp.concatenate(per_sub, axis=0)

    def prod(factors):
        out = None
        for f in factors:
            out = f if out is None else out * f
        return out

    def scaled(block, factors):
        f = prod(factors)
        return block if f is None else block * f

    cum_in = [dx[n_low * c:(n_low + 1) * c] for dx in decays]
    cum_out = [dx[(n_low + 1) * c:] for dx in decays]
    whole = [e[c - 1:c, :] for e in cum_in]
    level_decay = [stack([dx[lvl * c:(lvl + 1) * c] for dx in decays]) for lvl in reversed(range(n_low))]
    g = 1
    while g < n_sub:
        per_sub = []
        for x in range(n_sub):
            mid_sub = x - x % (2 * g) + g
            if x >= mid_sub:
                per_sub.append(scaled(cum_in[x], whole[mid_sub:x]))
            else:
                per_sub.append(scaled(cum_out[x], whole[x + 1:mid_sub]))
        level_decay.append(stack(per_sub))
        g *= 2
    e_in = stack([scaled(cum_in[x], whole[:x]) for x in range(n_sub)])
    e_out = stack([scaled(cum_out[x], whole[x + 1:]) for x in range(n_sub)])

    t_idx = lax.broadcasted_iota(jnp.int32, (rows_all, rows_all), 0)
    s_idx = lax.broadcasted_iota(jnp.int32, (rows_all, rows_all), 1)
    x_idx = t_idx ^ s_idx
    scores = jnp.zeros((rows_all, rows_all), F32)
    for lvl, e in enumerate(level_decay):
        s_l = lax.dot_general((q * e).astype(BF16), (k * e).astype(BF16), _NT, preferred_element_type=F32)
        scores = jnp.where(x_idx >= (1 << lvl), s_l, scores)
    diag = jnp.where(t_idx == s_idx, jnp.sum(q * k, axis=-1, keepdims=True), 0.0)
    scores = jnp.where(t_idx > s_idx, scores, diag)

    o = jnp.dot(scores.astype(BF16), v16, preferred_element_type=F32)
    o = o + lax.dot_general((q * e_in).astype(BF16), st.astype(BF16), _NT, preferred_element_type=F32)
    st = st * prod(whole) + lax.dot_general(v16, (k * e_out).astype(BF16), _TN, preferred_element_type=F32)
    o = o * lax.rsqrt(jnp.mean(o * o, axis=-1, keepdims=True) + EPS) * gn
    return o * gate, st


def _hgrn_kernel(*refs, layer, chunk, n_sub, groups_per_iter, n_iters, heads_per_step):
    q_ref, f_ref, i_ref, g_ref, lbl_ref, gn_ref, p_ref, s0_ref = refs[:8]
    o_ref, s_ref, st_ref = refs[-3:]
    step = pl.program_id(2)

    @pl.when(step == 0)
    def _():
        for hh in range(heads_per_step):
            st_ref[hh] = s0_ref[0, hh].T

    logits = lbl_ref[...]
    ex = jnp.exp(logits - jnp.max(logits, axis=0, keepdims=True))
    sm = ex / jnp.sum(ex, axis=0, keepdims=True)
    lb = jnp.zeros_like(sm[0:1])
    for r in range(1, layer + 1):
        lb = lb + sm[r:r + 1]
    gn = gn_ref[...]
    p = p_ref[...]
    group = n_sub * chunk

    def body(it, carry):
        for hh in range(heads_per_step):
            lanes = slice(hh * HEAD, (hh + 1) * HEAD)
            st = st_ref[hh]
            for g in range(groups_per_iter):
                start = (it * groups_per_iter + g) * group
                rows = pl.ds(start if isinstance(start, int) else pl.multiple_of(start, group), group)
                o, st = _hgrn_block(q_ref[rows, lanes], f_ref[rows, lanes], i_ref[rows, lanes],
                                    g_ref[rows, lanes], lb[:, lanes], gn[:, lanes], p, st, chunk, n_sub)
                o_ref[rows, lanes] = o.astype(o_ref.dtype)
            st_ref[hh] = st
        return carry

    if n_iters == 1:
        body(0, 0)
    else:
        lax.fori_loop(0, n_iters, body, 0)

    @pl.when(step == pl.num_programs(2) - 1)
    def _():
        for hh in range(heads_per_step):
            s_ref[0, hh] = st_ref[hh].T.astype(s_ref.dtype)


def hgrn_mixer(z, lb_logits, gnorm, s0, layer, row0, seq, chunk, rows_per_step, n_sub, groups_per_iter,
               heads_per_step, dest):
    d = z.shape[1] // 4
    width = heads_per_step * HEAD
    groups = d // width
    batch = s0.shape[0]
    steps = seq // rows_per_step
    blk0 = row0 // rows_per_step
    rows_per_iter = chunk * n_sub * groups_per_iter
    assert row0 % rows_per_step == 0 and seq % rows_per_step == 0 and rows_per_step % rows_per_iter == 0
    p = jnp.asarray(_hgrn_sum_matrix(chunk), BF16)

    def col(part):
        return pl.BlockSpec((rows_per_step, width), lambda b, h, s: (blk0 + b * steps + s, part * groups + h))

    head_row = lambda b, h, s: (0, h)
    state_blk = pl.BlockSpec((1, heads_per_step, HEAD, HEAD), lambda b, h, s: (b, h, 0, 0))
    out_shape, dest_spec = _mixer_out(dest, d)
    return pl.pallas_call(
        functools.partial(_hgrn_kernel, layer=layer, chunk=chunk, n_sub=n_sub, groups_per_iter=groups_per_iter,
                          n_iters=rows_per_step // rows_per_iter, heads_per_step=heads_per_step),
        grid=(batch, groups, steps),
        in_specs=[col(0), col(1), col(2), col(3),
                  pl.BlockSpec((lb_logits.shape[0], width), head_row),
                  pl.BlockSpec((1, width), head_row),
                  pl.BlockSpec(p.shape, lambda b, h, s: (0, 0)),
                  state_blk, dest_spec],
        out_specs=[pl.BlockSpec((rows_per_step, width), lambda b, h, s: (blk0 + b * steps + s, h)), state_blk],
        out_shape=[out_shape, jax.ShapeDtypeStruct(s0.shape, F32)],
        input_output_aliases={8: 0},
        scratch_shapes=[pltpu.VMEM((heads_per_step, HEAD, HEAD), F32)],
        compiler_params=_params("arbitrary", "arbitrary", "arbitrary"),
        name="hgrn_mixer",
    )(z, z, z, z, lb_logits, gnorm.reshape(1, d), p, s0, dest)


def _later_keys_matrix(tk):
    j = lax.broadcasted_iota(jnp.int32, (tk, tk), 0)
    s = lax.broadcasted_iota(jnp.int32, (tk, tk), 1)
    return jnp.where(j > s, 1.0, 0.0).astype(BF16)


def _sb_tile(q, ks, vs, later, acc, run, key_offset):
    z = lax.dot_general(q, ks, _NT, preferred_element_type=F32).astype(BF16)
    log_beta = jnp.minimum(z, 0) - jnp.log(1 + jnp.exp(-jnp.abs(z)))
    log_keep = log_beta - z
    if key_offset is not None:
        mask = (lax.broadcasted_iota(jnp.int32, z.shape, 1) + key_offset
                < lax.broadcasted_iota(jnp.int32, z.shape, 0))
        log_keep = jnp.where(mask, log_keep, jnp.zeros_like(log_keep))
    after = jnp.dot(log_keep, later, preferred_element_type=F32)
    w = jnp.exp(log_beta.astype(F32) + after + run)
    if key_offset is not None:
        w = jnp.where(mask, w, 0.0)
    acc = acc + jnp.dot(w.astype(BF16), vs, preferred_element_type=F32)
    run = run + jnp.sum(log_keep.astype(F32), axis=-1, keepdims=True)
    return acc, run


def _sb_prompt_kernel(q_ref, k_ref, v_ref, dest_ref, o_ref, *, tq, tk):
    del dest_ref
    qi = pl.program_id(2)
    q = q_ref[...]
    later = _later_keys_matrix(tk)
    per_q = tq // tk

    def visit(j, carry, key_offset):
        rows = pl.ds(j * tk if isinstance(j, int) else pl.multiple_of(j * tk, tk), tk)
        return _sb_tile(q, k_ref[rows, :], v_ref[rows, :], later, carry[0], carry[1], key_offset)

    carry = (jnp.zeros((tq, HEAD), F32), jnp.zeros((tq, 1), F32))
    for dgl in reversed(range(per_q)):
        carry = visit(qi * per_q + dgl, carry, dgl * tk)

    def quad(t, c):
        j = qi * per_q - 1 - 4 * t
        for u in range(4):
            c = visit(j - u, c, None)
        return c

    n_left = qi * per_q
    carry = lax.fori_loop(0, n_left // 4, quad, carry)
    carry = lax.cond(n_left % 4 != 0, lambda c: visit(0, visit(1, c, None), None), lambda c: c, carry)
    o_ref[...] = carry[0].astype(o_ref.dtype)


def sb_prompt(q, kv16, batch, seq, dest, tq=SB_QUERY_TILE, tk=SB_KEY_TILE):
    d = q.shape[1]
    heads = d // HEAD
    nq = seq // tq
    assert seq % tq == 0 and tq == 2 * tk
    out_shape, dest_spec = _mixer_out(dest, d)
    return pl.pallas_call(
        functools.partial(_sb_prompt_kernel, tq=tq, tk=tk),
        grid=(batch, heads, nq),
        in_specs=[pl.BlockSpec((tq, HEAD), lambda b, h, i: (b * nq + i, h)),
                  pl.BlockSpec((seq, HEAD), lambda b, h, i: (b, h)),
                  pl.BlockSpec((seq, HEAD), lambda b, h, i: (b, heads + h)),
                  dest_spec],
        out_specs=pl.BlockSpec((tq, HEAD), lambda b, h, i: (b * nq + i, h)),
        out_shape=out_shape,
        input_output_aliases={3: 0},
        compiler_params=_params("arbitrary", "arbitrary", "arbitrary"),
        name="sb_prompt",
    )(q, kv16, kv16, dest)


def _cache_rows_kernel(ck_hbm, cv_hbm, ok_ref, ov_ref, buf, sem, *, heads_per_step, groups):
    n = pl.program_id(0) * groups + pl.program_id(1)
    steps = pl.num_programs(0) * groups

    def copies(step, slot):
        b, g = step // groups, step % groups
        return [pltpu.make_async_copy(src.at[b, :, g * heads_per_step + hh, :], buf.at[slot, a, hh],
                                      sem.at[slot, a, hh])
                for a, src in enumerate((ck_hbm, cv_hbm)) for hh in range(heads_per_step)]

    @pl.when(n == 0)
    def _():
        for c in copies(n, 0):
            c.start()

    @pl.when(n + 1 < steps)
    def _():
        for c in copies(n + 1, (n + 1) % 2):
            c.start()

    slot = n % 2
    for c in copies(n, slot):
        c.wait()
    for a, o_ref in enumerate((ok_ref, ov_ref)):
        for hh in range(heads_per_step):
            o_ref[:, hh * HEAD:(hh + 1) * HEAD] = buf[slot, a, hh].astype(o_ref.dtype)


def cache_rows(cache_k, cache_v, heads_per_step=SAMPLE_HEADS_PER_STEP):
    batch, past, heads, head = cache_k.shape
    assert head == HEAD and heads % heads_per_step == 0 and cache_v.shape == cache_k.shape
    groups = heads // heads_per_step
    out = jax.ShapeDtypeStruct((batch, past, heads * head), BF16)
    blk = pl.BlockSpec((None, past, heads_per_step * head), lambda b, g: (b, 0, g))
    return pl.pallas_call(
        functools.partial(_cache_rows_kernel, heads_per_step=heads_per_step, groups=groups),
        grid=(batch, groups),
        in_specs=[pl.BlockSpec(memory_space=pl.ANY), pl.BlockSpec(memory_space=pl.ANY)],
        out_specs=[blk, blk],
        out_shape=[out, out],
        scratch_shapes=[pltpu.VMEM((2, 2, heads_per_step, past, head), F32),
                        pltpu.SemaphoreType.DMA((2, 2, heads_per_step))],
        compiler_params=_params("arbitrary", "arbitrary"),
        name="cache_rows",
    )(cache_k, cache_v)


def _sb_sample_kernel(q_ref, k_ref, v_ref, ck_ref, cv_ref, dest_ref, o_ref, *, tk, heads_per_step):
    del dest_ref
    t = q_ref.shape[0]
    past = ck_ref.shape[0]
    later_new = _later_keys_matrix(t)
    later = _later_keys_matrix(tk)
    for hh in range(heads_per_step):
        lanes = slice(hh * HEAD, (hh + 1) * HEAD)
        q = q_ref[:, lanes]
        carry = (jnp.zeros((t, HEAD), F32), jnp.zeros((t, 1), F32))
        carry = _sb_tile(q, k_ref[:, lanes], v_ref[:, lanes], later_new, carry[0], carry[1], 0)
        for j in range(past // tk - 1, -1, -1):
            rows = slice(j * tk, (j + 1) * tk)
            carry = _sb_tile(q, ck_ref[rows, lanes], cv_ref[rows, lanes], later, carry[0], carry[1], None)
        o_ref[:, lanes] = carry[0].astype(o_ref.dtype)


def sb_sample(q, kv16, cache_k, cache_v, row0, batch, seq, dest, tk=SB_KEY_TILE,
              heads_per_step=SAMPLE_HEADS_PER_STEP):
    d = q.shape[1]
    width = heads_per_step * HEAD
    groups = d // width
    past = cache_k.shape[1]
    blk0 = row0 // seq
    assert row0 % seq == 0 and past % tk == 0
    own = lambda off: pl.BlockSpec((seq, width), lambda b, h: (blk0 + b, off + h))
    cache_blk = pl.BlockSpec((None, past, width), lambda b, h: (b, 0, h))
    out_shape, dest_spec = _mixer_out(dest, d)
    return pl.pallas_call(
        functools.partial(_sb_sample_kernel, tk=tk, heads_per_step=heads_per_step),
        grid=(batch, groups),
        in_specs=[own(0), own(0), own(groups), cache_blk, cache_blk, dest_spec],
        out_specs=own(0),
        out_shape=out_shape,
        input_output_aliases={5: 0},
        compiler_params=_params("arbitrary", "arbitrary"),
        name="sb_sample",
    )(q, kv16, kv16, cache_k, cache_v, dest)


def kernel(x_prompt, x_sample, state_hgrn, cache_k, cache_v, norm_mix, norm_ffn, norm_kv, norm_out,
           a_w_in, a_lb_logits, a_gnorm, a_w_out, b_w_kv, b_w_q, b_w_out, ffn_w_gate_up, ffn_w_down):
    pb, ps, d = x_prompt.shape
    sb, ss, _ = x_sample.shape
    n_a = a_w_in.shape[0]
    depth = norm_mix.shape[0]
    heads = d // HEAD
    mp, ms = pb * ps, sb * ss
    m = mp + ms

    x = jnp.concatenate([x_prompt.reshape(mp, d), x_sample.reshape(ms, d)], axis=0)
    zero_state = jnp.zeros((pb, heads, HEAD, HEAD), F32)
    w_down16 = ffn_w_down.astype(BF16)
    states_p, states_s = [], []
    kv32 = kv16 = ck16 = cv16 = h_kv = None
    o = jnp.zeros((m, d), BF16)
    h, rnorm = rmsnorm(x, norm_mix[0], BF16, 384), None
    for layer in range(depth):
        if layer < n_a:
            z = matmul(h, a_w_in, layer, rnorm)
            o, s_p = hgrn_mixer(z, a_lb_logits, a_gnorm[layer], zero_state, layer, 0, ps,
                                HGRN_CHUNK, 64 * HGRN_CHUNK, 4, 16, 1, o)
            o, s_s = hgrn_mixer(z, a_lb_logits, a_gnorm[layer], state_hgrn[layer], layer, mp, ss,
                                ss, ss, 1, 1, SAMPLE_HEADS_PER_STEP, o)
            states_p.append(s_p)
            states_s.append(s_s)
            x, (h_ffn,), rnorm_ffn = matmul(o, a_w_out, layer, res=x, gains=(norm_ffn[layer],))
        else:
            j = layer - n_a
            if j == 0:
                kv32, kv16 = matmul(h_kv, b_w_kv, None, rnorm, out_dtypes=(F32, BF16))
                ck16, cv16 = cache_rows(cache_k, cache_v)
            q = matmul(h, b_w_q, j, rnorm, out_dtypes=(BF16,), out_scale=SB_SCALE)
            o = sb_prompt(q, kv16, pb, ps, o)
            o = sb_sample(q, kv16, ck16, cv16, mp, sb, ss, o)
            x, (h_ffn,), rnorm_ffn = matmul(o, b_w_out, j, res=x, gains=(norm_ffn[layer],))
        hidden = gate_up(h_ffn, ffn_w_gate_up, layer, rnorm_ffn)
        if layer + 1 == depth:
            x = matmul(hidden, w_down16, layer, res=x, tm=1056)
        elif layer + 1 == n_a:
            x, (h, h_kv), rnorm = matmul(hidden, w_down16, layer, res=x, gains=(norm_mix[layer + 1], norm_kv),
                                         tm=1056)
        else:
            x, (h,), rnorm = matmul(hidden, w_down16, layer, res=x, gains=(norm_mix[layer + 1],), tm=1056)
    y_p = rmsnorm(x, norm_out, F32, 256, 0, mp)
    y_s = rmsnorm(x, norm_out, F32, 256, mp, ms)

    k_new, v_new = kv32[:, :d], kv32[:, d:]
    return (y_p.reshape(pb, ps, d), y_s.reshape(sb, ss, d),
            jnp.stack(states_p).astype(x.dtype), k_new[:mp].reshape(pb, ps, heads, HEAD),
            v_new[:mp].reshape(pb, ps, heads, HEAD),
            jnp.stack(states_s).astype(x.dtype), k_new[mp:].reshape(sb, ss, heads, HEAD),
            v_new[mp:].reshape(sb, ss, heads, HEAD))
```

```python
import functools

import numpy as np
import jax
import jax.numpy as jnp
from jax import lax
from jax.experimental import pallas as pl
from jax.experimental.pallas import tpu as pltpu

F32 = jnp.float32
BF16 = jnp.bfloat16
EPS = 1e-6
HEAD = 128
HGRN_CHUNK = 64
SB_KEY_TILE = 256
SB_QUERY_TILE = 2 * SB_KEY_TILE
SB_SCALE = HEAD ** -0.5
SAMPLE_HEADS_PER_STEP = 8
LANES = 128
V7X_VMEM_LIMIT_BYTES = 58 * 1024 * 1024

_NT = (((1,), (1,)), ((), ()))
_TN = (((0,), (0,)), ((), ()))


def _params(*sem):
    return pltpu.CompilerParams(dimension_semantics=sem, vmem_limit_bytes=V7X_VMEM_LIMIT_BYTES)


def _rmsnorm_kernel(x_ref, g_ref, o_ref):
    x = x_ref[...]
    y = x * lax.rsqrt(jnp.mean(x * x, axis=-1, keepdims=True) + EPS)
    o_ref[...] = (y * g_ref[...]).astype(o_ref.dtype)


def rmsnorm(x, g, out_dtype, rows, row0=0, nrows=None):
    d = x.shape[1]
    nrows = x.shape[0] if nrows is None else nrows
    assert nrows % rows == 0 and row0 % rows == 0
    blk0 = row0 // rows
    return pl.pallas_call(
        _rmsnorm_kernel,
        grid=(nrows // rows,),
        in_specs=[pl.BlockSpec((rows, d), lambda i: (blk0 + i, 0)), pl.BlockSpec((1, d), lambda i: (0, 0))],
        out_specs=pl.BlockSpec((rows, d), lambda i: (i, 0)),
        out_shape=jax.ShapeDtypeStruct((nrows, d), out_dtype),
        compiler_params=_params("arbitrary"),
        name="rmsnorm",
    )(x, g.reshape(1, d))


def _mxu_operand(w_ref):
    w = w_ref[...]
    return w if w.dtype == BF16 else w.astype(BF16)


def _row_scaled(acc, rnorm_ref):
    return acc * jnp.tile(rnorm_ref[...], (1, acc.shape[1] // LANES))


def _matmul_kernel(*refs, has_rnorm, has_res, n_out, n_gain, out_scale, out_width):
    refs = list(refs)
    a_ref, w_ref = refs.pop(0), refs.pop(0)
    rnorm_ref = refs.pop(0) if has_rnorm else None
    res_ref = refs.pop(0) if has_res else None
    gain_refs = [refs.pop(0) for _ in range(n_gain)]
    out_refs = [refs.pop(0) for _ in range(n_out)]
    normed_refs = [refs.pop(0) for _ in range(n_gain)]
    acc = jnp.dot(a_ref[...], _mxu_operand(w_ref), preferred_element_type=F32)
    if has_rnorm:
        acc = _row_scaled(acc, rnorm_ref)
    if out_scale is not None:
        acc = acc * out_scale
    if has_res:
        acc = res_ref[...] + acc
    for o_ref in out_refs:
        o_ref[...] = acc.astype(o_ref.dtype)
    if n_gain:
        stat_ref = refs.pop(0)
        j = pl.program_id(1)
        for g_ref, o_ref in zip(gain_refs, normed_refs):
            o_ref[...] = (acc * g_ref[...]).astype(o_ref.dtype)
        sq = acc * acc
        part = sq[:, :LANES]
        for c0 in range(LANES, sq.shape[1], LANES):
            part = part + sq[:, c0:c0 + LANES]

        @pl.when(j == 0)
        def _():
            stat_ref[...] = part

        @pl.when(j != 0)
        def _():
            stat_ref[...] = stat_ref[...] + part

        @pl.when(j == pl.num_programs(1) - 1)
        def _():
            mean = jnp.sum(stat_ref[...], axis=-1, keepdims=True) / out_width
            stat_ref[...] = jnp.broadcast_to(lax.rsqrt(mean + EPS), stat_ref.shape)


def _resident_rows(tm, k):
    return pl.BlockSpec((tm, k), lambda i, j: (i, 0), pipeline_mode=pl.Buffered(1))


def _weight_cols(w, layer, tn, col0=0):
    if w.ndim == 2:
        return pl.BlockSpec((w.shape[0], tn), lambda i, j: (0, j + col0))
    return pl.BlockSpec((None, w.shape[1], tn), lambda i, j: (layer, 0, j + col0))


def _row_stat(tm):
    return pl.BlockSpec((tm, LANES), lambda i, j: (i, 0))


def matmul(a, w, layer=None, rnorm=None, res=None, out_dtypes=(F32,), out_scale=None, gains=(), tm=2112, tn=256):
    m, k = a.shape
    n = w.shape[-1]
    assert m % tm == 0 and n % tn == 0 and tn % LANES == 0 and w.shape[-2] == k
    tile = pl.BlockSpec((tm, tn), lambda i, j: (i, j))
    in_specs = [_resident_rows(tm, k), _weight_cols(w, layer, tn)]
    args = [a, w]
    if rnorm is not None:
        in_specs.append(_row_stat(tm))
        args.append(rnorm)
    if res is not None:
        in_specs.append(tile)
        args.append(res)
    for g in gains:
        in_specs.append(pl.BlockSpec((1, tn), lambda i, j: (0, j)))
        args.append(g.reshape(1, n))
    out_specs = [tile for _ in out_dtypes] + [tile for _ in gains]
    out_shape = ([jax.ShapeDtypeStruct((m, n), dt) for dt in out_dtypes]
                 + [jax.ShapeDtypeStruct((m, n), BF16) for _ in gains])
    if gains:
        out_specs.append(_row_stat(tm))
        out_shape.append(jax.ShapeDtypeStruct((m, LANES), F32))
    outs = pl.pallas_call(
        functools.partial(_matmul_kernel, has_rnorm=rnorm is not None, has_res=res is not None,
                          n_out=len(out_dtypes), n_gain=len(gains), out_scale=out_scale, out_width=n),
        grid=(m // tm, n // tn),
        in_specs=in_specs,
        out_specs=out_specs,
        out_shape=out_shape,
        compiler_params=_params("arbitrary", "arbitrary"),
        name="matmul",
    )(*args)
    main = outs[0] if len(out_dtypes) == 1 else outs[:len(out_dtypes)]
    if not gains:
        return main
    return main, outs[len(out_dtypes):-1], outs[-1]


def _gate_up_kernel(a_ref, wg_ref, wu_ref, rnorm_ref, wd_ref, o_ref, wd16_ref):
    a = a_ref[...]
    gate = _row_scaled(jnp.dot(a, _mxu_operand(wg_ref), preferred_element_type=F32), rnorm_ref)
    up = _row_scaled(jnp.dot(a, _mxu_operand(wu_ref), preferred_element_type=F32), rnorm_ref)
    o_ref[...] = (gate * jax.nn.sigmoid(gate) * up).astype(o_ref.dtype)
    wd16_ref[...] = wd_ref[...].astype(wd16_ref.dtype)


def gate_up(a, w, w_down, layer, rnorm, tm=2112, tn=256):
    m, k = a.shape
    f = w.shape[-1] // 2
    n = w_down.shape[-1]
    nj = f // tn
    steps = (m // tm) * nj
    slab = f // steps
    assert m % tm == 0 and f % tn == 0 and tn % LANES == 0 and w.shape[-2] == k
    assert w_down.shape[-2] == f and f % steps == 0 and slab % 16 == 0
    outs = pl.pallas_call(
        _gate_up_kernel,
        grid=(m // tm, nj),
        in_specs=[_resident_rows(tm, k), _weight_cols(w, layer, tn), _weight_cols(w, layer, tn, nj), _row_stat(tm),
                  pl.BlockSpec((None, slab, n), lambda i, j: (layer, i * nj + j, 0))],
        out_specs=[pl.BlockSpec((tm, tn), lambda i, j: (i, j)),
                   pl.BlockSpec((slab, n), lambda i, j: (i * nj + j, 0))],
        out_shape=[jax.ShapeDtypeStruct((m, f), BF16), jax.ShapeDtypeStruct((f, n), BF16)],
        compiler_params=_params("arbitrary", "arbitrary"),
        name="gate_up",
    )(a, w, w, rnorm, w_down)
    return outs


def _mixer_out(dest, d):
    assert dest.shape[1] == d and dest.dtype == BF16
    return jax.ShapeDtypeStruct(dest.shape, BF16), pl.BlockSpec(memory_space=pl.ANY)


def _hgrn_sum_matrix(c):
    r = np.arange(c)[:, None]
    u = np.arange(c)[None, :]
    blocks = []
    n = c // 2
    while n >= 1:
        m = (r // (2 * n)) * (2 * n) + n - 1
        blocks.append(np.where((r & n) != 0, (u > m) & (u <= r), (u > r) & (u <= m)))
        n //= 2
    blocks.append(u <= r)
    blocks.append(u > r)
    p = np.concatenate(blocks, axis=0).astype(np.float32)
    return np.concatenate([p, p, p, np.zeros_like(p)], axis=1)


def _hgrn_block(qz, fz, iz, gz, lb, gn, p, st, c, n_sub):
    rows_all = n_sub * c
    n_low = c.bit_length() - 1
    q = qz * jax.nn.sigmoid(qz)
    forget = lb + (1.0 - lb) * jax.nn.sigmoid(fz)
    log_f = jnp.log(forget)
    k = 1.0 - forget
    v16 = iz.astype(BF16)
    gate = gz * jax.nn.sigmoid(gz)

    hi = log_f.astype(BF16)
    rem = log_f - hi.astype(F32)
    mid = rem.astype(BF16)
    lo = (rem - mid.astype(F32)).astype(BF16)
    zero = jnp.zeros((c, HEAD), BF16)
    pieces = [jnp.concatenate([hi[x * c:(x + 1) * c], mid[x * c:(x + 1) * c], lo[x * c:(x + 1) * c], zero],
                              axis=0) for x in range(n_sub)]
    decays = []
    for x in range(0, n_sub - 1, 2):
        pair = jnp.exp(jnp.dot(p, jnp.concatenate(pieces[x:x + 2], axis=1), preferred_element_type=F32))
        decays += [pair[:, :HEAD], pair[:, HEAD:]]
    if n_sub % 2:
        decays.append(jnp.exp(jnp.dot(p, pieces[-1], preferred_element_type=F32)))

    def stack(per_sub):
        return per_sub[0] if n_sub == 1 else jnp.concatenate(per_sub, axis=0)

    def prod(factors):
        out = None
        for f in factors:
            out = f if out is None else out * f
        return out

    def scaled(block, factors):
        f = prod(factors)
        return block if f is None else block * f

    cum_in = [dx[n_low * c:(n_low + 1) * c] for dx in decays]
    cum_out = [dx[(n_low + 1) * c:] for dx in decays]
    whole = [e[c - 1:c, :] for e in cum_in]
    level_decay = [stack([dx[lvl * c:(lvl + 1) * c] for dx in decays]) for lvl in reversed(range(n_low))]
    g = 1
    while g < n_sub:
        per_sub = []
        for x in range(n_sub):
            mid_sub = x - x % (2 * g) + g
            if x >= mid_sub:
                per_sub.append(scaled(cum_in[x], whole[mid_sub:x]))
            else:
                per_sub.append(scaled(cum_out[x], whole[x + 1:mid_sub]))
        level_decay.append(stack(per_sub))
        g *= 2
    e_in = stack([scaled(cum_in[x], whole[:x]) for x in range(n_sub)])
    e_out = stack([scaled(cum_out[x], whole[x + 1:]) for x in range(n_sub)])

    t_idx = lax.broadcasted_iota(jnp.int32, (rows_all, rows_all), 0)
    s_idx = lax.broadcasted_iota(jnp.int32, (rows_all, rows_all), 1)
    x_idx = t_idx ^ s_idx
    scores = jnp.zeros((rows_all, rows_all), F32)
    for lvl, e in enumerate(level_decay):
        s_l = lax.dot_general((q * e).astype(BF16), (k * e).astype(BF16), _NT, preferred_element_type=F32)
        scores = jnp.where(x_idx >= (1 << lvl), s_l, scores)
    diag = jnp.where(t_idx == s_idx, jnp.sum(q * k, axis=-1, keepdims=True), 0.0)
    scores = jnp.where(t_idx > s_idx, scores, diag)

    o = jnp.dot(scores.astype(BF16), v16, preferred_element_type=F32)
    o = o + lax.dot_general((q * e_in).astype(BF16), st.astype(BF16), _NT, preferred_element_type=F32)
    st = st * prod(whole) + lax.dot_general(v16, (k * e_out).astype(BF16), _TN, preferred_element_type=F32)
    o = o * lax.rsqrt(jnp.mean(o * o, axis=-1, keepdims=True) + EPS) * gn
    return o * gate, st


def _hgrn_kernel(*refs, layer, chunk, n_sub, groups_per_iter, n_iters, heads_per_step):
    q_ref, f_ref, i_ref, g_ref, lbl_ref, gn_ref, p_ref, s0_ref = refs[:8]
    o_ref, s_ref, st_ref = refs[-3:]
    step = pl.program_id(2)

    @pl.when(step == 0)
    def _():
        for hh in range(heads_per_step):
            st_ref[hh] = s0_ref[0, hh].T

    logits = lbl_ref[...]
    ex = jnp.exp(logits - jnp.max(logits, axis=0, keepdims=True))
    sm = ex / jnp.sum(ex, axis=0, keepdims=True)
    lb = jnp.zeros_like(sm[0:1])
    for r in range(1, layer + 1):
        lb = lb + sm[r:r + 1]
    gn = gn_ref[...]
    p = p_ref[...]
    group = n_sub * chunk

    def body(it, carry):
        for hh in range(heads_per_step):
            lanes = slice(hh * HEAD, (hh + 1) * HEAD)
            st = st_ref[hh]
            for g in range(groups_per_iter):
                start = (it * groups_per_iter + g) * group
                rows = pl.ds(start if isinstance(start, int) else pl.multiple_of(start, group), group)
                o, st = _hgrn_block(q_ref[rows, lanes], f_ref[rows, lanes], i_ref[rows, lanes],
                                    g_ref[rows, lanes], lb[:, lanes], gn[:, lanes], p, st, chunk, n_sub)
                o_ref[rows, lanes] = o.astype(o_ref.dtype)
            st_ref[hh] = st
        return carry

    if n_iters == 1:
        body(0, 0)
    else:
        lax.fori_loop(0, n_iters, body, 0)

    @pl.when(step == pl.num_programs(2) - 1)
    def _():
        for hh in range(heads_per_step):
            s_ref[0, hh] = st_ref[hh].T.astype(s_ref.dtype)


def hgrn_mixer(z, lb_logits, gnorm, s0, layer, row0, seq, chunk, rows_per_step, n_sub, groups_per_iter,
               heads_per_step, dest):
    d = z.shape[1] // 4
    width = heads_per_step * HEAD
    groups = d // width
    batch = s0.shape[0]
    steps = seq // rows_per_step
    blk0 = row0 // rows_per_step
    rows_per_iter = chunk * n_sub * groups_per_iter
    assert row0 % rows_per_step == 0 and seq % rows_per_step == 0 and rows_per_step % rows_per_iter == 0
    p = jnp.asarray(_hgrn_sum_matrix(chunk), BF16)

    def col(part):
        return pl.BlockSpec((rows_per_step, width), lambda b, h, s: (blk0 + b * steps + s, part * groups + h))

    head_row = lambda b, h, s: (0, h)
    state_blk = pl.BlockSpec((1, heads_per_step, HEAD, HEAD), lambda b, h, s: (b, h, 0, 0))
    out_shape, dest_spec = _mixer_out(dest, d)
    return pl.pallas_call(
        functools.partial(_hgrn_kernel, layer=layer, chunk=chunk, n_sub=n_sub, groups_per_iter=groups_per_iter,
                          n_iters=rows_per_step // rows_per_iter, heads_per_step=heads_per_step),
        grid=(batch, groups, steps),
        in_specs=[col(0), col(1), col(2), col(3),
                  pl.BlockSpec((lb_logits.shape[0], width), head_row),
                  pl.BlockSpec((1, width), head_row),
                  pl.BlockSpec(p.shape, lambda b, h, s: (0, 0)),
                  state_blk, dest_spec],
        out_specs=[pl.BlockSpec((rows_per_step, width), lambda b, h, s: (blk0 + b * steps + s, h)), state_blk],
        out_shape=[out_shape, jax.ShapeDtypeStruct(s0.shape, F32)],
        input_output_aliases={8: 0},
        scratch_shapes=[pltpu.VMEM((heads_per_step, HEAD, HEAD), F32)],
        compiler_params=_params("arbitrary", "arbitrary", "arbitrary"),
        name="hgrn_mixer",
    )(z, z, z, z, lb_logits, gnorm.reshape(1, d), p, s0, dest)


def _later_keys_matrix(tk):
    j = lax.broadcasted_iota(jnp.int32, (tk, tk), 0)
    s = lax.broadcasted_iota(jnp.int32, (tk, tk), 1)
    return jnp.where(j > s, 1.0, 0.0).astype(BF16)


def _sb_weights(z, later, run, key_offset, rows_per_head=None):
    z = z.astype(BF16)
    log_beta = jnp.minimum(z, 0) - jnp.log(1 + jnp.exp(-jnp.abs(z)))
    log_keep = log_beta - z
    if key_offset is not None:
        t_idx = lax.broadcasted_iota(jnp.int32, z.shape, 0)
        if rows_per_head is not None:
            t_idx = lax.rem(t_idx, rows_per_head)
        mask = lax.broadcasted_iota(jnp.int32, z.shape, 1) + key_offset < t_idx
        log_keep = jnp.where(mask, log_keep, jnp.zeros_like(log_keep))
    after = jnp.dot(log_keep, later, preferred_element_type=F32)
    w = jnp.exp(log_beta.astype(F32) + after + run)
    if key_offset is not None:
        w = jnp.where(mask, w, 0.0)
    return w.astype(BF16), run + jnp.sum(log_keep.astype(F32), axis=-1, keepdims=True)


def _sb_tile(q, ks, vs, later, acc, run, key_offset):
    w, run = _sb_weights(lax.dot_general(q, ks, _NT, preferred_element_type=F32), later, run, key_offset)
    return acc + jnp.dot(w, vs, preferred_element_type=F32), run


def _sb_prompt_kernel(q_ref, k_ref, v_ref, dest_ref, o_ref, *, tq, tk):
    del dest_ref
    qi = pl.program_id(2)
    q = q_ref[...]
    later = _later_keys_matrix(tk)
    per_q = tq // tk

    def visit(j, carry, key_offset):
        rows = pl.ds(j * tk if isinstance(j, int) else pl.multiple_of(j * tk, tk), tk)
        return _sb_tile(q, k_ref[rows, :], v_ref[rows, :], later, carry[0], carry[1], key_offset)

    carry = (jnp.zeros((tq, HEAD), F32), jnp.zeros((tq, 1), F32))
    for dgl in reversed(range(per_q)):
        carry = visit(qi * per_q + dgl, carry, dgl * tk)

    def quad(t, c):
        j = qi * per_q - 1 - 4 * t
        for u in range(4):
            c = visit(j - u, c, None)
        return c

    n_left = qi * per_q
    carry = lax.fori_loop(0, n_left // 4, quad, carry)
    carry = lax.cond(n_left % 4 != 0, lambda c: visit(0, visit(1, c, None), None), lambda c: c, carry)
    o_ref[...] = carry[0].astype(o_ref.dtype)


def sb_prompt(q, kv16, batch, seq, dest, tq=SB_QUERY_TILE, tk=SB_KEY_TILE):
    d = q.shape[1]
    heads = d // HEAD
    nq = seq // tq
    assert seq % tq == 0 and tq == 2 * tk
    out_shape, dest_spec = _mixer_out(dest, d)
    return pl.pallas_call(
        functools.partial(_sb_prompt_kernel, tq=tq, tk=tk),
        grid=(batch, heads, nq),
        in_specs=[pl.BlockSpec((tq, HEAD), lambda b, h, i: (b * nq + i, h)),
                  pl.BlockSpec((seq, HEAD), lambda b, h, i: (b, h)),
                  pl.BlockSpec((seq, HEAD), lambda b, h, i: (b, heads + h)),
                  dest_spec],
        out_specs=pl.BlockSpec((tq, HEAD), lambda b, h, i: (b * nq + i, h)),
        out_shape=out_shape,
        input_output_aliases={3: 0},
        compiler_params=_params("arbitrary", "arbitrary", "arbitrary"),
        name="sb_prompt",
    )(q, kv16, kv16, dest)


def _cache_rows_kernel(ck_hbm, cv_hbm, ok_ref, ov_ref, buf, sem, *, heads_per_step, groups):
    n = pl.program_id(0) * groups + pl.program_id(1)
    steps = pl.num_programs(0) * groups

    def copies(step, slot):
        b, g = step // groups, step % groups
        return [pltpu.make_async_copy(src.at[b, :, g * heads_per_step + hh, :], buf.at[slot, a, hh],
                                      sem.at[slot, a, hh])
                for a, src in enumerate((ck_hbm, cv_hbm)) for hh in range(heads_per_step)]

    @pl.when(n == 0)
    def _():
        for c in copies(n, 0):
            c.start()

    @pl.when(n + 1 < steps)
    def _():
        for c in copies(n + 1, (n + 1) % 2):
            c.start()

    slot = n % 2
    for c in copies(n, slot):
        c.wait()
    for a, o_ref in enumerate((ok_ref, ov_ref)):
        for hh in range(heads_per_step):
            o_ref[:, hh * HEAD:(hh + 1) * HEAD] = buf[slot, a, hh].astype(o_ref.dtype)


def cache_rows(cache_k, cache_v, heads_per_step=SAMPLE_HEADS_PER_STEP):
    batch, past, heads, head = cache_k.shape
    assert head == HEAD and heads % heads_per_step == 0 and cache_v.shape == cache_k.shape
    groups = heads // heads_per_step
    out = jax.ShapeDtypeStruct((batch, past, heads * head), BF16)
    blk = pl.BlockSpec((None, past, heads_per_step * head), lambda b, g: (b, 0, g))
    return pl.pallas_call(
        functools.partial(_cache_rows_kernel, heads_per_step=heads_per_step, groups=groups),
        grid=(batch, groups),
        in_specs=[pl.BlockSpec(memory_space=pl.ANY), pl.BlockSpec(memory_space=pl.ANY)],
        out_specs=[blk, blk],
        out_shape=[out, out],
        scratch_shapes=[pltpu.VMEM((2, 2, heads_per_step, past, head), F32),
                        pltpu.SemaphoreType.DMA((2, 2, heads_per_step))],
        compiler_params=_params("arbitrary", "arbitrary"),
        name="cache_rows",
    )(cache_k, cache_v)


def _sb_sample_kernel(q_ref, k_ref, v_ref, ck_ref, cv_ref, dest_ref, o_ref, *, tk, heads_per_step):
    del dest_ref
    t = q_ref.shape[0]
    past = ck_ref.shape[0]
    lanes = [slice(hh * HEAD, (hh + 1) * HEAD) for hh in range(heads_per_step)]
    qs = [q_ref[:, ln] for ln in lanes]

    def visit(k_tile, v_tile, later, accs, run, key_offset):
        z = jnp.concatenate([lax.dot_general(q, k_tile(ln), _NT, preferred_element_type=F32)
                             for q, ln in zip(qs, lanes)], axis=0)
        w, run = _sb_weights(z, later, run, key_offset, t)
        accs = [acc + jnp.dot(w[hh * t:(hh + 1) * t], v_tile(ln), preferred_element_type=F32)
                for hh, (acc, ln) in enumerate(zip(accs, lanes))]
        return accs, run

    accs = [jnp.zeros((t, HEAD), F32) for _ in lanes]
    run = jnp.zeros((heads_per_step * t, 1), F32)
    accs, run = visit(lambda ln: k_ref[:, ln], lambda ln: v_ref[:, ln], _later_keys_matrix(t), accs, run, 0)
    later = _later_keys_matrix(tk)
    for j in range(past // tk - 1, -1, -1):
        rows = slice(j * tk, (j + 1) * tk)
        accs, run = visit(lambda ln: ck_ref[rows, ln], lambda ln: cv_ref[rows, ln], later, accs, run, None)
    for acc, ln in zip(accs, lanes):
        o_ref[:, ln] = acc.astype(o_ref.dtype)


def sb_sample(q, kv16, cache_k, cache_v, row0, batch, seq, dest, tk=SB_KEY_TILE,
              heads_per_step=SAMPLE_HEADS_PER_STEP):
    d = q.shape[1]
    width = heads_per_step * HEAD
    groups = d // width
    past = cache_k.shape[1]
    blk0 = row0 // seq
    assert row0 % seq == 0 and past % tk == 0
    own = lambda off: pl.BlockSpec((seq, width), lambda b, h: (blk0 + b, off + h))
    cache_blk = pl.BlockSpec((None, past, width), lambda b, h: (b, 0, h))
    out_shape, dest_spec = _mixer_out(dest, d)
    return pl.pallas_call(
        functools.partial(_sb_sample_kernel, tk=tk, heads_per_step=heads_per_step),
        grid=(batch, groups),
        in_specs=[own(0), own(0), own(groups), cache_blk, cache_blk, dest_spec],
        out_specs=own(0),
        out_shape=out_shape,
        input_output_aliases={5: 0},
        compiler_params=_params("arbitrary", "arbitrary"),
        name="sb_sample",
    )(q, kv16, kv16, cache_k, cache_v, dest)


def kernel(x_prompt, x_sample, state_hgrn, cache_k, cache_v, norm_mix, norm_ffn, norm_kv, norm_out,
           a_w_in, a_lb_logits, a_gnorm, a_w_out, b_w_kv, b_w_q, b_w_out, ffn_w_gate_up, ffn_w_down):
    pb, ps, d = x_prompt.shape
    sb, ss, _ = x_sample.shape
    n_a = a_w_in.shape[0]
    depth = norm_mix.shape[0]
    heads = d // HEAD
    mp, ms = pb * ps, sb * ss
    m = mp + ms

    x = jnp.concatenate([x_prompt.reshape(mp, d), x_sample.reshape(ms, d)], axis=0)
    zero_state = jnp.zeros((pb, heads, HEAD, HEAD), F32)
    states_p, states_s = [], []
    kv32 = kv16 = ck16 = cv16 = h_kv = None
    o = jnp.zeros((m, d), BF16)
    h, rnorm = rmsnorm(x, norm_mix[0], BF16, 384), None
    for layer in range(depth):
        if layer < n_a:
            z = matmul(h, a_w_in, layer, rnorm)
            o, s_p = hgrn_mixer(z, a_lb_logits, a_gnorm[layer], zero_state, layer, 0, ps,
                                HGRN_CHUNK, 64 * HGRN_CHUNK, 4, 16, 1, o)
            o, s_s = hgrn_mixer(z, a_lb_logits, a_gnorm[layer], state_hgrn[layer], layer, mp, ss,
                                ss, ss, 1, 1, SAMPLE_HEADS_PER_STEP, o)
            states_p.append(s_p)
            states_s.append(s_s)
            x, (h_ffn,), rnorm_ffn = matmul(o, a_w_out, layer, res=x, gains=(norm_ffn[layer],))
        else:
            j = layer - n_a
            if j == 0:
                kv32, kv16 = matmul(h_kv, b_w_kv, None, rnorm, out_dtypes=(F32, BF16))
                ck16, cv16 = cache_rows(cache_k, cache_v)
            q = matmul(h, b_w_q, j, rnorm, out_dtypes=(BF16,), out_scale=SB_SCALE)
            o = sb_prompt(q, kv16, pb, ps, o)
            o = sb_sample(q, kv16, ck16, cv16, mp, sb, ss, o)
            x, (h_ffn,), rnorm_ffn = matmul(o, b_w_out, j, res=x, gains=(norm_ffn[layer],))
        hidden, w_down16 = gate_up(h_ffn, ffn_w_gate_up, ffn_w_down, layer, rnorm_ffn)
        if layer + 1 == depth:
            x = matmul(hidden, w_down16, res=x, tm=1056)
        elif layer + 1 == n_a:
            x, (h, h_kv), rnorm = matmul(hidden, w_down16, res=x, gains=(norm_mix[layer + 1], norm_kv), tm=1056)
        else:
            x, (h,), rnorm = matmul(hidden, w_down16, res=x, gains=(norm_mix[layer + 1],), tm=1056)
    y_p = rmsnorm(x, norm_out, F32, 256, 0, mp)
    y_s = rmsnorm(x, norm_out, F32, 256, mp, ms)

    k_new, v_new = kv32[:, :d], kv32[:, d:]
    return (y_p.reshape(pb, ps, d), y_s.reshape(sb, ss, d),
            jnp.stack(states_p).astype(x.dtype), k_new[:mp].reshape(pb, ps, heads, HEAD),
            v_new[:mp].reshape(pb, ps, heads, HEAD),
            jnp.stack(states_s).astype(x.dtype), k_new[mp:].reshape(sb, ss, heads, HEAD),
            v_new[mp:].reshape(sb, ss, heads, HEAD))
```

```python
import functools

import numpy as np
import jax
import jax.numpy as jnp
from jax import lax
from jax.experimental import pallas as pl
from jax.experimental.pallas import tpu as pltpu

F32 = jnp.float32
BF16 = jnp.bfloat16
EPS = 1e-6
HEAD = 128
HGRN_CHUNK = 64
SB_KEY_TILE = 256
SB_QUERY_TILE = 2 * SB_KEY_TILE
SB_SCALE = HEAD ** -0.5
SAMPLE_HEADS_PER_STEP = 8
LANES = 128
V7X_VMEM_LIMIT_BYTES = 58 * 1024 * 1024

_NT = (((1,), (1,)), ((), ()))
_TN = (((0,), (0,)), ((), ()))


def _params(*sem):
    return pltpu.CompilerParams(dimension_semantics=sem, vmem_limit_bytes=V7X_VMEM_LIMIT_BYTES)


def _rmsnorm_kernel(x_ref, g_ref, o_ref):
    x = x_ref[...]
    y = x * lax.rsqrt(jnp.mean(x * x, axis=-1, keepdims=True) + EPS)
    o_ref[...] = (y * g_ref[...]).astype(o_ref.dtype)


def rmsnorm(x, g, out_dtype, rows, row0=0, nrows=None):
    d = x.shape[1]
    nrows = x.shape[0] if nrows is None else nrows
    assert nrows % rows == 0 and row0 % rows == 0
    blk0 = row0 // rows
    return pl.pallas_call(
        _rmsnorm_kernel,
        grid=(nrows // rows,),
        in_specs=[pl.BlockSpec((rows, d), lambda i: (blk0 + i, 0)), pl.BlockSpec((1, d), lambda i: (0, 0))],
        out_specs=pl.BlockSpec((rows, d), lambda i: (i, 0)),
        out_shape=jax.ShapeDtypeStruct((nrows, d), out_dtype),
        compiler_params=_params("arbitrary"),
        name="rmsnorm",
    )(x, g.reshape(1, d))


def _mxu_operand(w_ref):
    w = w_ref[...]
    return w if w.dtype == BF16 else w.astype(BF16)


def _row_scaled(acc, rnorm_ref):
    return acc * jnp.tile(rnorm_ref[...], (1, acc.shape[1] // LANES))


def _matmul_kernel(*refs, has_rnorm, has_res, n_out, n_gain, out_scale, out_width):
    refs = list(refs)
    a_ref, w_ref = refs.pop(0), refs.pop(0)
    rnorm_ref = refs.pop(0) if has_rnorm else None
    res_ref = refs.pop(0) if has_res else None
    gain_refs = [refs.pop(0) for _ in range(n_gain)]
    out_refs = [refs.pop(0) for _ in range(n_out)]
    normed_refs = [refs.pop(0) for _ in range(n_gain)]
    acc = jnp.dot(a_ref[...], _mxu_operand(w_ref), preferred_element_type=F32)
    if has_rnorm:
        acc = _row_scaled(acc, rnorm_ref)
    if out_scale is not None:
        acc = acc * out_scale
    if has_res:
        acc = res_ref[...] + acc
    for o_ref in out_refs:
        o_ref[...] = acc.astype(o_ref.dtype)
    if n_gain:
        stat_ref = refs.pop(0)
        j = pl.program_id(1)
        for g_ref, o_ref in zip(gain_refs, normed_refs):
            o_ref[...] = (acc * g_ref[...]).astype(o_ref.dtype)
        sq = acc * acc
        part = sq[:, :LANES]
        for c0 in range(LANES, sq.shape[1], LANES):
            part = part + sq[:, c0:c0 + LANES]

        @pl.when(j == 0)
        def _():
            stat_ref[...] = part

        @pl.when(j != 0)
        def _():
            stat_ref[...] = stat_ref[...] + part

        @pl.when(j == pl.num_programs(1) - 1)
        def _():
            mean = jnp.sum(stat_ref[...], axis=-1, keepdims=True) / out_width
            stat_ref[...] = jnp.broadcast_to(lax.rsqrt(mean + EPS), stat_ref.shape)


def _resident_rows(tm, k):
    return pl.BlockSpec((tm, k), lambda i, j: (i, 0), pipeline_mode=pl.Buffered(1))


def _weight_cols(w, layer, tn, col0=0):
    if w.ndim == 2:
        return pl.BlockSpec((w.shape[0], tn), lambda i, j: (0, j + col0))
    return pl.BlockSpec((None, w.shape[1], tn), lambda i, j: (layer, 0, j + col0))


def _row_stat(tm):
    return pl.BlockSpec((tm, LANES), lambda i, j: (i, 0))


def matmul(a, w, layer=None, rnorm=None, res=None, out_dtypes=(F32,), out_scale=None, gains=(), tm=2112, tn=256):
    m, k = a.shape
    n = w.shape[-1]
    assert m % tm == 0 and n % tn == 0 and tn % LANES == 0 and w.shape[-2] == k
    tile = pl.BlockSpec((tm, tn), lambda i, j: (i, j))
    in_specs = [_resident_rows(tm, k), _weight_cols(w, layer, tn)]
    args = [a, w]
    if rnorm is not None:
        in_specs.append(_row_stat(tm))
        args.append(rnorm)
    if res is not None:
        in_specs.append(tile)
        args.append(res)
    for g in gains:
        in_specs.append(pl.BlockSpec((1, tn), lambda i, j: (0, j)))
        args.append(g.reshape(1, n))
    out_specs = [tile for _ in out_dtypes] + [tile for _ in gains]
    out_shape = ([jax.ShapeDtypeStruct((m, n), dt) for dt in out_dtypes]
                 + [jax.ShapeDtypeStruct((m, n), BF16) for _ in gains])
    if gains:
        out_specs.append(_row_stat(tm))
        out_shape.append(jax.ShapeDtypeStruct((m, LANES), F32))
    outs = pl.pallas_call(
        functools.partial(_matmul_kernel, has_rnorm=rnorm is not None, has_res=res is not None,
                          n_out=len(out_dtypes), n_gain=len(gains), out_scale=out_scale, out_width=n),
        grid=(m // tm, n // tn),
        in_specs=in_specs,
        out_specs=out_specs,
        out_shape=out_shape,
        compiler_params=_params("arbitrary", "arbitrary"),
        name="matmul",
    )(*args)
    main = outs[0] if len(out_dtypes) == 1 else outs[:len(out_dtypes)]
    if not gains:
        return main
    return main, outs[len(out_dtypes):-1], outs[-1]


def _gate_up_kernel(a_ref, wg_ref, wu_ref, rnorm_ref, wd_ref, o_ref, wd16_ref):
    a = a_ref[...]
    gate = _row_scaled(jnp.dot(a, _mxu_operand(wg_ref), preferred_element_type=F32), rnorm_ref)
    up = _row_scaled(jnp.dot(a, _mxu_operand(wu_ref), preferred_element_type=F32), rnorm_ref)
    o_ref[...] = (gate * jax.nn.sigmoid(gate) * up).astype(o_ref.dtype)
    wd16_ref[...] = wd_ref[...].astype(wd16_ref.dtype)


def gate_up(a, w, w_down, layer, rnorm, tm=2112, tn=256):
    m, k = a.shape
    f = w.shape[-1] // 2
    n = w_down.shape[-1]
    nj = f // tn
    steps = (m // tm) * nj
    slab = f // steps
    assert m % tm == 0 and f % tn == 0 and tn % LANES == 0 and w.shape[-2] == k
    assert w_down.shape[-2] == f and f % steps == 0 and slab % 16 == 0
    outs = pl.pallas_call(
        _gate_up_kernel,
        grid=(m // tm, nj),
        in_specs=[_resident_rows(tm, k), _weight_cols(w, layer, tn), _weight_cols(w, layer, tn, nj), _row_stat(tm),
                  pl.BlockSpec((None, slab, n), lambda i, j: (layer, i * nj + j, 0))],
        out_specs=[pl.BlockSpec((tm, tn), lambda i, j: (i, j)),
                   pl.BlockSpec((slab, n), lambda i, j: (i * nj + j, 0))],
        out_shape=[jax.ShapeDtypeStruct((m, f), BF16), jax.ShapeDtypeStruct((f, n), BF16)],
        compiler_params=_params("arbitrary", "arbitrary"),
        name="gate_up",
    )(a, w, w, rnorm, w_down)
    return outs


def _mixer_out(dest, d):
    assert dest.shape[1] == d and dest.dtype == BF16
    return jax.ShapeDtypeStruct(dest.shape, BF16), pl.BlockSpec(memory_space=pl.ANY)


def _hgrn_sum_matrix(c):
    r = np.arange(c)[:, None]
    u = np.arange(c)[None, :]
    blocks = []
    n = c // 2
    while n >= 1:
        m = (r // (2 * n)) * (2 * n) + n - 1
        blocks.append(np.where((r & n) != 0, (u > m) & (u <= r), (u > r) & (u <= m)))
        n //= 2
    blocks.append(u <= r)
    blocks.append(u > r)
    p = np.concatenate(blocks, axis=0).astype(np.float32)
    return np.concatenate([p, p, p, np.zeros_like(p)], axis=1)


def _hgrn_block(qz, fz, iz, gz, lb, gn, p, st, c, n_sub):
    rows_all = n_sub * c
    n_low = c.bit_length() - 1
    q = qz * jax.nn.sigmoid(qz)
    forget = lb + (1.0 - lb) * jax.nn.sigmoid(fz)
    log_f = jnp.log(forget)
    k = 1.0 - forget
    v16 = iz.astype(BF16)
    gate = gz * jax.nn.sigmoid(gz)

    hi = log_f.astype(BF16)
    rem = log_f - hi.astype(F32)
    mid = rem.astype(BF16)
    lo = (rem - mid.astype(F32)).astype(BF16)
    zero = jnp.zeros((c, HEAD), BF16)
    pieces = [jnp.concatenate([hi[x * c:(x + 1) * c], mid[x * c:(x + 1) * c], lo[x * c:(x + 1) * c], zero],
                              axis=0) for x in range(n_sub)]
    decays = []
    for x in range(0, n_sub - 1, 2):
        pair = jnp.exp(jnp.dot(p, jnp.concatenate(pieces[x:x + 2], axis=1), preferred_element_type=F32))
        decays += [pair[:, :HEAD], pair[:, HEAD:]]
    if n_sub % 2:
        decays.append(jnp.exp(jnp.dot(p, pieces[-1], preferred_element_type=F32)))

    def stack(per_sub):
        return per_sub[0] if n_sub == 1 else jnp.concatenate(per_sub, axis=0)

    def prod(factors):
        out = None
        for f in factors:
            out = f if out is None else out * f
        return out

    def scaled(block, factors):
        f = prod(factors)
        return block if f is None else block * f

    cum_in = [dx[n_low * c:(n_low + 1) * c] for dx in decays]
    cum_out = [dx[(n_low + 1) * c:] for dx in decays]
    whole = [e[c - 1:c, :] for e in cum_in]
    level_decay = [stack([dx[lvl * c:(lvl + 1) * c] for dx in decays]) for lvl in reversed(range(n_low))]
    g = 1
    while g < n_sub:
        per_sub = []
        for x in range(n_sub):
            mid_sub = x - x % (2 * g) + g
            if x >= mid_sub:
                per_sub.append(scaled(cum_in[x], whole[mid_sub:x]))
            else:
                per_sub.append(scaled(cum_out[x], whole[x + 1:mid_sub]))
        level_decay.append(stack(per_sub))
        g *= 2
    e_in = stack([scaled(cum_in[x], whole[:x]) for x in range(n_sub)])
    e_out = stack([scaled(cum_out[x], whole[x + 1:]) for x in range(n_sub)])

    t_idx = lax.broadcasted_iota(jnp.int32, (rows_all, rows_all), 0)
    s_idx = lax.broadcasted_iota(jnp.int32, (rows_all, rows_all), 1)
    x_idx = t_idx ^ s_idx
    scores = jnp.zeros((rows_all, rows_all), F32)
    for lvl, e in enumerate(level_decay):
        s_l = lax.dot_general((q * e).astype(BF16), (k * e).astype(BF16), _NT, preferred_element_type=F32)
        scores = jnp.where(x_idx >= (1 << lvl), s_l, scores)
    diag = jnp.where(t_idx == s_idx, jnp.sum(q * k, axis=-1, keepdims=True), 0.0)
    scores = jnp.where(t_idx > s_idx, scores, diag)

    o = jnp.dot(scores.astype(BF16), v16, preferred_element_type=F32)
    o = o + lax.dot_general((q * e_in).astype(BF16), st.astype(BF16), _NT, preferred_element_type=F32)
    st = st * prod(whole) + lax.dot_general(v16, (k * e_out).astype(BF16), _TN, preferred_element_type=F32)
    o = o * lax.rsqrt(jnp.mean(o * o, axis=-1, keepdims=True) + EPS) * gn
    return o * gate, st


def _hgrn_kernel(*refs, layer, chunk, n_sub, groups_per_iter, n_iters, heads_per_step):
    q_ref, f_ref, i_ref, g_ref, lbl_ref, gn_ref, p_ref, s0_ref = refs[:8]
    o_ref, s_ref, st_ref = refs[-3:]
    step = pl.program_id(2)

    @pl.when(step == 0)
    def _():
        for hh in range(heads_per_step):
            st_ref[hh] = s0_ref[0, hh].T

    logits = lbl_ref[...]
    ex = jnp.exp(logits - jnp.max(logits, axis=0, keepdims=True))
    sm = ex / jnp.sum(ex, axis=0, keepdims=True)
    lb = jnp.zeros_like(sm[0:1])
    for r in range(1, layer + 1):
        lb = lb + sm[r:r + 1]
    gn = gn_ref[...]
    p = p_ref[...]
    group = n_sub * chunk

    def body(it, carry):
        for hh in range(heads_per_step):
            lanes = slice(hh * HEAD, (hh + 1) * HEAD)
            st = st_ref[hh]
            for g in range(groups_per_iter):
                start = (it * groups_per_iter + g) * group
                rows = pl.ds(start if isinstance(start, int) else pl.multiple_of(start, group), group)
                o, st = _hgrn_block(q_ref[rows, lanes], f_ref[rows, lanes], i_ref[rows, lanes],
                                    g_ref[rows, lanes], lb[:, lanes], gn[:, lanes], p, st, chunk, n_sub)
                o_ref[rows, lanes] = o.astype(o_ref.dtype)
            st_ref[hh] = st
        return carry

    if n_iters == 1:
        body(0, 0)
    else:
        lax.fori_loop(0, n_iters, body, 0)

    @pl.when(step == pl.num_programs(2) - 1)
    def _():
        for hh in range(heads_per_step):
            s_ref[0, hh] = st_ref[hh].T.astype(s_ref.dtype)


def hgrn_mixer(z, lb_logits, gnorm, s0, layer, row0, seq, chunk, rows_per_step, n_sub, groups_per_iter,
               heads_per_step, dest):
    d = z.shape[1] // 4
    width = heads_per_step * HEAD
    groups = d // width
    batch = s0.shape[0]
    steps = seq // rows_per_step
    blk0 = row0 // rows_per_step
    rows_per_iter = chunk * n_sub * groups_per_iter
    assert row0 % rows_per_step == 0 and seq % rows_per_step == 0 and rows_per_step % rows_per_iter == 0
    p = jnp.asarray(_hgrn_sum_matrix(chunk), BF16)

    def col(part):
        return pl.BlockSpec((rows_per_step, width), lambda b, h, s: (blk0 + b * steps + s, part * groups + h))

    head_row = lambda b, h, s: (0, h)
    state_blk = pl.BlockSpec((1, heads_per_step, HEAD, HEAD), lambda b, h, s: (b, h, 0, 0))
    out_shape, dest_spec = _mixer_out(dest, d)
    return pl.pallas_call(
        functools.partial(_hgrn_kernel, layer=layer, chunk=chunk, n_sub=n_sub, groups_per_iter=groups_per_iter,
                          n_iters=rows_per_step // rows_per_iter, heads_per_step=heads_per_step),
        grid=(batch, groups, steps),
        in_specs=[col(0), col(1), col(2), col(3),
                  pl.BlockSpec((lb_logits.shape[0], width), head_row),
                  pl.BlockSpec((1, width), head_row),
                  pl.BlockSpec(p.shape, lambda b, h, s: (0, 0)),
                  state_blk, dest_spec],
        out_specs=[pl.BlockSpec((rows_per_step, width), lambda b, h, s: (blk0 + b * steps + s, h)), state_blk],
        out_shape=[out_shape, jax.ShapeDtypeStruct(s0.shape, F32)],
        input_output_aliases={8: 0},
        scratch_shapes=[pltpu.VMEM((heads_per_step, HEAD, HEAD), F32)],
        compiler_params=_params("arbitrary", "arbitrary", "arbitrary"),
        name="hgrn_mixer",
    )(z, z, z, z, lb_logits, gnorm.reshape(1, d), p, s0, dest)


def _later_keys_matrix(tk):
    j = lax.broadcasted_iota(jnp.int32, (tk, tk), 0)
    s = lax.broadcasted_iota(jnp.int32, (tk, tk), 1)
    return jnp.where(j > s, 1.0, 0.0).astype(BF16)


def _sb_weights(z, later, run, key_offset, rows_per_head=None):
    z = z.astype(BF16)
    log_beta = jnp.minimum(z, 0) - jnp.log(1 + jnp.exp(-jnp.abs(z)))
    log_keep = log_beta - z
    if key_offset is not None:
        t_idx = lax.broadcasted_iota(jnp.int32, z.shape, 0)
        if rows_per_head is not None:
            t_idx = lax.rem(t_idx, rows_per_head)
        mask = lax.broadcasted_iota(jnp.int32, z.shape, 1) + key_offset < t_idx
        log_keep = jnp.where(mask, log_keep, jnp.zeros_like(log_keep))
    after = jnp.dot(log_keep, later, preferred_element_type=F32)
    w = jnp.exp(log_beta.astype(F32) + after + run)
    if key_offset is not None:
        w = jnp.where(mask, w, 0.0)
    return w.astype(BF16), run + jnp.sum(log_keep.astype(F32), axis=-1, keepdims=True)


def _sb_tile(q, ks, vs, later, acc, run, key_offset):
    w, run = _sb_weights(lax.dot_general(q, ks, _NT, preferred_element_type=F32), later, run, key_offset)
    return acc + jnp.dot(w, vs, preferred_element_type=F32), run


def _sb_prompt_kernel(q_ref, k_ref, v_ref, dest_ref, o_ref, *, tq, tk):
    del dest_ref
    qi = pl.program_id(2)
    q = q_ref[...]
    later = _later_keys_matrix(tk)
    per_q = tq // tk

    def visit(j, carry, key_offset):
        rows = pl.ds(j * tk if isinstance(j, int) else pl.multiple_of(j * tk, tk), tk)
        return _sb_tile(q, k_ref[rows, :], v_ref[rows, :], later, carry[0], carry[1], key_offset)

    carry = (jnp.zeros((tq, HEAD), F32), jnp.zeros((tq, 1), F32))
    for dgl in reversed(range(per_q)):
        r0 = dgl * tk
        rows = pl.ds(pl.multiple_of((qi * per_q + dgl) * tk, tk), tk)
        acc, run = _sb_tile(q[r0:], k_ref[rows, :], v_ref[rows, :], later, carry[0][r0:], carry[1][r0:], 0)
        if r0:
            acc = jnp.concatenate([carry[0][:r0], acc], axis=0)
            run = jnp.concatenate([carry[1][:r0], run], axis=0)
        carry = (acc, run)

    def quad(t, c):
        j = qi * per_q - 1 - 4 * t
        for u in range(4):
            c = visit(j - u, c, None)
        return c

    n_left = qi * per_q
    carry = lax.fori_loop(0, n_left // 4, quad, carry)
    carry = lax.cond(n_left % 4 != 0, lambda c: visit(0, visit(1, c, None), None), lambda c: c, carry)
    o_ref[...] = carry[0].astype(o_ref.dtype)


def sb_prompt(q, kv16, batch, seq, dest, tq=SB_QUERY_TILE, tk=SB_KEY_TILE):
    d = q.shape[1]
    heads = d // HEAD
    nq = seq // tq
    assert seq % tq == 0 and tq == 2 * tk
    out_shape, dest_spec = _mixer_out(dest, d)
    return pl.pallas_call(
        functools.partial(_sb_prompt_kernel, tq=tq, tk=tk),
        grid=(batch, heads, nq),
        in_specs=[pl.BlockSpec((tq, HEAD), lambda b, h, i: (b * nq + i, h)),
                  pl.BlockSpec((seq, HEAD), lambda b, h, i: (b, h)),
                  pl.BlockSpec((seq, HEAD), lambda b, h, i: (b, heads + h)),
                  dest_spec],
        out_specs=pl.BlockSpec((tq, HEAD), lambda b, h, i: (b * nq + i, h)),
        out_shape=out_shape,
        input_output_aliases={3: 0},
        compiler_params=_params("arbitrary", "arbitrary", "arbitrary"),
        name="sb_prompt",
    )(q, kv16, kv16, dest)


def _cache_rows_kernel(ck_hbm, cv_hbm, ok_ref, ov_ref, buf, sem, *, heads_per_step, groups):
    n = pl.program_id(0) * groups + pl.program_id(1)
    steps = pl.num_programs(0) * groups

    def copies(step, slot):
        b, g = step // groups, step % groups
        return [pltpu.make_async_copy(src.at[b, :, g * heads_per_step + hh, :], buf.at[slot, a, hh],
                                      sem.at[slot, a, hh])
                for a, src in enumerate((ck_hbm, cv_hbm)) for hh in range(heads_per_step)]

    @pl.when(n == 0)
    def _():
        for c in copies(n, 0):
            c.start()

    @pl.when(n + 1 < steps)
    def _():
        for c in copies(n + 1, (n + 1) % 2):
            c.start()

    slot = n % 2
    for c in copies(n, slot):
        c.wait()
    for a, o_ref in enumerate((ok_ref, ov_ref)):
        for hh in range(heads_per_step):
            o_ref[:, hh * HEAD:(hh + 1) * HEAD] = buf[slot, a, hh].astype(o_ref.dtype)


def cache_rows(cache_k, cache_v, heads_per_step=SAMPLE_HEADS_PER_STEP):
    batch, past, heads, head = cache_k.shape
    assert head == HEAD and heads % heads_per_step == 0 and cache_v.shape == cache_k.shape
    groups = heads // heads_per_step
    out = jax.ShapeDtypeStruct((batch, past, heads * head), BF16)
    blk = pl.BlockSpec((None, past, heads_per_step * head), lambda b, g: (b, 0, g))
    return pl.pallas_call(
        functools.partial(_cache_rows_kernel, heads_per_step=heads_per_step, groups=groups),
        grid=(batch, groups),
        in_specs=[pl.BlockSpec(memory_space=pl.ANY), pl.BlockSpec(memory_space=pl.ANY)],
        out_specs=[blk, blk],
        out_shape=[out, out],
        scratch_shapes=[pltpu.VMEM((2, 2, heads_per_step, past, head), F32),
                        pltpu.SemaphoreType.DMA((2, 2, heads_per_step))],
        compiler_params=_params("arbitrary", "arbitrary"),
        name="cache_rows",
    )(cache_k, cache_v)


def _sb_sample_kernel(q_ref, k_ref, v_ref, ck_ref, cv_ref, dest_ref, o_ref, *, tk, heads_per_step):
    del dest_ref
    t = q_ref.shape[0]
    past = ck_ref.shape[0]
    lanes = [slice(hh * HEAD, (hh + 1) * HEAD) for hh in range(heads_per_step)]
    qs = [q_ref[:, ln] for ln in lanes]

    def visit(k_tile, v_tile, later, accs, run, key_offset):
        z = jnp.concatenate([lax.dot_general(q, k_tile(ln), _NT, preferred_element_type=F32)
                             for q, ln in zip(qs, lanes)], axis=0)
        w, run = _sb_weights(z, later, run, key_offset, t)
        accs = [acc + jnp.dot(w[hh * t:(hh + 1) * t], v_tile(ln), preferred_element_type=F32)
                for hh, (acc, ln) in enumerate(zip(accs, lanes))]
        return accs, run

    accs = [jnp.zeros((t, HEAD), F32) for _ in lanes]
    run = jnp.zeros((heads_per_step * t, 1), F32)
    accs, run = visit(lambda ln: k_ref[:, ln], lambda ln: v_ref[:, ln], _later_keys_matrix(t), accs, run, 0)
    later = _later_keys_matrix(tk)
    for j in range(past // tk - 1, -1, -1):
        rows = slice(j * tk, (j + 1) * tk)
        accs, run = visit(lambda ln: ck_ref[rows, ln], lambda ln: cv_ref[rows, ln], later, accs, run, None)
    for acc, ln in zip(accs, lanes):
        o_ref[:, ln] = acc.astype(o_ref.dtype)


def sb_sample(q, kv16, cache_k, cache_v, row0, batch, seq, dest, tk=SB_KEY_TILE,
              heads_per_step=SAMPLE_HEADS_PER_STEP):
    d = q.shape[1]
    width = heads_per_step * HEAD
    groups = d // width
    past = cache_k.shape[1]
    blk0 = row0 // seq
    assert row0 % seq == 0 and past % tk == 0
    own = lambda off: pl.BlockSpec((seq, width), lambda b, h: (blk0 + b, off + h))
    cache_blk = pl.BlockSpec((None, past, width), lambda b, h: (b, 0, h))
    out_shape, dest_spec = _mixer_out(dest, d)
    return pl.pallas_call(
        functools.partial(_sb_sample_kernel, tk=tk, heads_per_step=heads_per_step),
        grid=(batch, groups),
        in_specs=[own(0), own(0), own(groups), cache_blk, cache_blk, dest_spec],
        out_specs=own(0),
        out_shape=out_shape,
        input_output_aliases={5: 0},
        compiler_params=_params("arbitrary", "arbitrary"),
        name="sb_sample",
    )(q, kv16, kv16, cache_k, cache_v, dest)


def kernel(x_prompt, x_sample, state_hgrn, cache_k, cache_v, norm_mix, norm_ffn, norm_kv, norm_out,
           a_w_in, a_lb_logits, a_gnorm, a_w_out, b_w_kv, b_w_q, b_w_out, ffn_w_gate_up, ffn_w_down):
    pb, ps, d = x_prompt.shape
    sb, ss, _ = x_sample.shape
    n_a = a_w_in.shape[0]
    depth = norm_mix.shape[0]
    heads = d // HEAD
    mp, ms = pb * ps, sb * ss
    m = mp + ms

    x = jnp.concatenate([x_prompt.reshape(mp, d), x_sample.reshape(ms, d)], axis=0)
    zero_state = jnp.zeros((pb, heads, HEAD, HEAD), F32)
    states_p, states_s = [], []
    kv32 = kv16 = ck16 = cv16 = h_kv = None
    o = jnp.zeros((m, d), BF16)
    h, rnorm = rmsnorm(x, norm_mix[0], BF16, 384), None
    for layer in range(depth):
        if layer < n_a:
            z = matmul(h, a_w_in, layer, rnorm, tn=512)
            o, s_p = hgrn_mixer(z, a_lb_logits, a_gnorm[layer], zero_state, layer, 0, ps,
                                HGRN_CHUNK, 64 * HGRN_CHUNK, 4, 16, 1, o)
            o, s_s = hgrn_mixer(z, a_lb_logits, a_gnorm[layer], state_hgrn[layer], layer, mp, ss,
                                ss, ss, 1, 1, SAMPLE_HEADS_PER_STEP, o)
            states_p.append(s_p)
            states_s.append(s_s)
            x, (h_ffn,), rnorm_ffn = matmul(o, a_w_out, layer, res=x, gains=(norm_ffn[layer],))
        else:
            j = layer - n_a
            if j == 0:
                kv32, kv16 = matmul(h_kv, b_w_kv, None, rnorm, out_dtypes=(F32, BF16), tn=512)
                ck16, cv16 = cache_rows(cache_k, cache_v)
            q = matmul(h, b_w_q, j, rnorm, out_dtypes=(BF16,), out_scale=SB_SCALE, tn=512)
            o = sb_prompt(q, kv16, pb, ps, o)
            o = sb_sample(q, kv16, ck16, cv16, mp, sb, ss, o)
            x, (h_ffn,), rnorm_ffn = matmul(o, b_w_out, j, res=x, gains=(norm_ffn[layer],))
        hidden, w_down16 = gate_up(h_ffn, ffn_w_gate_up, ffn_w_down, layer, rnorm_ffn)
        if layer + 1 == depth:
            x = matmul(hidden, w_down16, res=x, tm=1056)
        elif layer + 1 == n_a:
            x, (h, h_kv), rnorm = matmul(hidden, w_down16, res=x, gains=(norm_mix[layer + 1], norm_kv), tm=1056)
        else:
            x, (h,), rnorm = matmul(hidden, w_down16, res=x, gains=(norm_mix[layer + 1],), tm=1056)
    y_p = rmsnorm(x, norm_out, F32, 256, 0, mp)
    y_s = rmsnorm(x, norm_out, F32, 256, mp, ms)

    k_new, v_new = kv32[:, :d], kv32[:, d:]
    return (y_p.reshape(pb, ps, d), y_s.reshape(sb, ss, d),
            jnp.stack(states_p).astype(x.dtype), k_new[:mp].reshape(pb, ps, heads, HEAD),
            v_new[:mp].reshape(pb, ps, heads, HEAD),
            jnp.stack(states_s).astype(x.dtype), k_new[mp:].reshape(sb, ss, heads, HEAD),
            v_new[mp:].reshape(sb, ss, heads, HEAD))
```

```python
import functools

import numpy as np
import jax
import jax.numpy as jnp
from jax import lax
from jax.experimental import pallas as pl
from jax.experimental.pallas import tpu as pltpu

F32 = jnp.float32
BF16 = jnp.bfloat16
EPS = 1e-6
HEAD = 128
HGRN_CHUNK = 64
SB_KEY_TILE = 256
SB_QUERY_TILE = 2 * SB_KEY_TILE
SB_SCALE = HEAD ** -0.5
SAMPLE_HEADS_PER_STEP = 8
LANES = 128
V7X_VMEM_LIMIT_BYTES = 58 * 1024 * 1024

_NT = (((1,), (1,)), ((), ()))
_TN = (((0,), (0,)), ((), ()))


def _params(*sem):
    return pltpu.CompilerParams(dimension_semantics=sem, vmem_limit_bytes=V7X_VMEM_LIMIT_BYTES)


def _rmsnorm_kernel(x_ref, g_ref, o_ref):
    x = x_ref[...]
    y = x * lax.rsqrt(jnp.mean(x * x, axis=-1, keepdims=True) + EPS)
    o_ref[...] = (y * g_ref[...]).astype(o_ref.dtype)


def rmsnorm(x, g, out_dtype, rows, row0=0, nrows=None):
    d = x.shape[1]
    nrows = x.shape[0] if nrows is None else nrows
    assert nrows % rows == 0 and row0 % rows == 0
    blk0 = row0 // rows
    return pl.pallas_call(
        _rmsnorm_kernel,
        grid=(nrows // rows,),
        in_specs=[pl.BlockSpec((rows, d), lambda i: (blk0 + i, 0)), pl.BlockSpec((1, d), lambda i: (0, 0))],
        out_specs=pl.BlockSpec((rows, d), lambda i: (i, 0)),
        out_shape=jax.ShapeDtypeStruct((nrows, d), out_dtype),
        compiler_params=_params("arbitrary"),
        name="rmsnorm",
    )(x, g.reshape(1, d))


def _mxu_operand(w_ref):
    w = w_ref[...]
    return w if w.dtype == BF16 else w.astype(BF16)


def _row_scaled(acc, rnorm_ref):
    return acc * jnp.tile(rnorm_ref[...], (1, acc.shape[1] // LANES))


def _matmul_kernel(*refs, has_rnorm, has_res, n_out, n_gain, out_scale, out_width):
    refs = list(refs)
    a_ref, w_ref = refs.pop(0), refs.pop(0)
    rnorm_ref = refs.pop(0) if has_rnorm else None
    res_ref = refs.pop(0) if has_res else None
    gain_refs = [refs.pop(0) for _ in range(n_gain)]
    out_refs = [refs.pop(0) for _ in range(n_out)]
    normed_refs = [refs.pop(0) for _ in range(n_gain)]
    acc = jnp.dot(a_ref[...], _mxu_operand(w_ref), preferred_element_type=F32)
    if has_rnorm:
        acc = _row_scaled(acc, rnorm_ref)
    if out_scale is not None:
        acc = acc * out_scale
    if has_res:
        acc = res_ref[...] + acc
    for o_ref in out_refs:
        o_ref[...] = acc.astype(o_ref.dtype)
    if n_gain:
        stat_ref = refs.pop(0)
        j = pl.program_id(1)
        for g_ref, o_ref in zip(gain_refs, normed_refs):
            o_ref[...] = (acc * g_ref[...]).astype(o_ref.dtype)
        sq = acc * acc
        part = sq[:, :LANES]
        for c0 in range(LANES, sq.shape[1], LANES):
            part = part + sq[:, c0:c0 + LANES]

        @pl.when(j == 0)
        def _():
            stat_ref[...] = part

        @pl.when(j != 0)
        def _():
            stat_ref[...] = stat_ref[...] + part

        @pl.when(j == pl.num_programs(1) - 1)
        def _():
            mean = jnp.sum(stat_ref[...], axis=-1, keepdims=True) / out_width
            stat_ref[...] = jnp.broadcast_to(lax.rsqrt(mean + EPS), stat_ref.shape)


def _resident_rows(tm, k):
    return pl.BlockSpec((tm, k), lambda i, j: (i, 0), pipeline_mode=pl.Buffered(1))


def _weight_cols(w, layer, tn, col0=0):
    if w.ndim == 2:
        return pl.BlockSpec((w.shape[0], tn), lambda i, j: (0, j + col0))
    return pl.BlockSpec((None, w.shape[1], tn), lambda i, j: (layer, 0, j + col0))


def _row_stat(tm):
    return pl.BlockSpec((tm, LANES), lambda i, j: (i, 0))


def matmul(a, w, layer=None, rnorm=None, res=None, out_dtypes=(F32,), out_scale=None, gains=(), tm=2112, tn=256):
    m, k = a.shape
    n = w.shape[-1]
    assert m % tm == 0 and n % tn == 0 and tn % LANES == 0 and w.shape[-2] == k
    tile = pl.BlockSpec((tm, tn), lambda i, j: (i, j))
    in_specs = [_resident_rows(tm, k), _weight_cols(w, layer, tn)]
    args = [a, w]
    if rnorm is not None:
        in_specs.append(_row_stat(tm))
        args.append(rnorm)
    if res is not None:
        in_specs.append(tile)
        args.append(res)
    for g in gains:
        in_specs.append(pl.BlockSpec((1, tn), lambda i, j: (0, j)))
        args.append(g.reshape(1, n))
    out_specs = [tile for _ in out_dtypes] + [tile for _ in gains]
    out_shape = ([jax.ShapeDtypeStruct((m, n), dt) for dt in out_dtypes]
                 + [jax.ShapeDtypeStruct((m, n), BF16) for _ in gains])
    if gains:
        out_specs.append(_row_stat(tm))
        out_shape.append(jax.ShapeDtypeStruct((m, LANES), F32))
    outs = pl.pallas_call(
        functools.partial(_matmul_kernel, has_rnorm=rnorm is not None, has_res=res is not None,
                          n_out=len(out_dtypes), n_gain=len(gains), out_scale=out_scale, out_width=n),
        grid=(m // tm, n // tn),
        in_specs=in_specs,
        out_specs=out_specs,
        out_shape=out_shape,
        compiler_params=_params("arbitrary", "arbitrary"),
        name="matmul",
    )(*args)
    main = outs[0] if len(out_dtypes) == 1 else outs[:len(out_dtypes)]
    if not gains:
        return main
    return main, outs[len(out_dtypes):-1], outs[-1]


def _head_major_copies(src_refs, dst_refs, sem):
    copies = []
    for a, (src, dst) in enumerate(zip(src_refs, dst_refs)):
        batch, _, heads, head = src.shape
        for b in range(batch):
            for h in range(heads):
                copies.append(pltpu.make_async_copy(src.at[b, :, h, :], dst.at[b, :, h * head:(h + 1) * head],
                                                    sem.at[a]))
    return copies


def _matmul_gather_kernel(a_ref, w_ref, ck_hbm, cv_hbm, o_ref, dk_hbm, dv_hbm, sem):
    i, j = pl.program_id(0), pl.program_id(1)
    copies = _head_major_copies((ck_hbm, cv_hbm), (dk_hbm, dv_hbm), sem)

    @pl.when((i == 0) & (j == 0))
    def _():
        for c in copies:
            c.start()

    o_ref[...] = jnp.dot(a_ref[...], _mxu_operand(w_ref), preferred_element_type=F32).astype(o_ref.dtype)

    @pl.when((i == pl.num_programs(0) - 1) & (j == pl.num_programs(1) - 1))
    def _():
        for c in copies:
            c.wait()


def matmul_with_cache_gather(a, w, layer, cache_k, cache_v, tm=2112, tn=512):
    m, k = a.shape
    n = w.shape[-1]
    batch, past, heads, head = cache_k.shape
    assert m % tm == 0 and n % tn == 0 and w.shape[-2] == k and cache_v.shape == cache_k.shape
    dense = jax.ShapeDtypeStruct((batch, past, heads * head), cache_k.dtype)
    hbm = pl.BlockSpec(memory_space=pl.ANY)
    return pl.pallas_call(
        _matmul_gather_kernel,
        grid=(m // tm, n // tn),
        in_specs=[_resident_rows(tm, k), _weight_cols(w, layer, tn), hbm, hbm],
        out_specs=[pl.BlockSpec((tm, tn), lambda i, j: (i, j)), hbm, hbm],
        out_shape=[jax.ShapeDtypeStruct((m, n), F32), dense, dense],
        scratch_shapes=[pltpu.SemaphoreType.DMA((2,))],
        compiler_params=_params("arbitrary", "arbitrary"),
        name="matmul_gather",
    )(a, w, cache_k, cache_v)


def _gate_up_kernel(a_ref, wg_ref, wu_ref, rnorm_ref, wd_ref, o_ref, wd16_ref):
    a = a_ref[...]
    gate = _row_scaled(jnp.dot(a, _mxu_operand(wg_ref), preferred_element_type=F32), rnorm_ref)
    up = _row_scaled(jnp.dot(a, _mxu_operand(wu_ref), preferred_element_type=F32), rnorm_ref)
    o_ref[...] = (gate * jax.nn.sigmoid(gate) * up).astype(o_ref.dtype)
    wd16_ref[...] = wd_ref[...].astype(wd16_ref.dtype)


def gate_up(a, w, w_down, layer, rnorm, tm=2112, tn=256):
    m, k = a.shape
    f = w.shape[-1] // 2
    n = w_down.shape[-1]
    nj = f // tn
    steps = (m // tm) * nj
    slab = f // steps
    assert m % tm == 0 and f % tn == 0 and tn % LANES == 0 and w.shape[-2] == k
    assert w_down.shape[-2] == f and f % steps == 0 and slab % 16 == 0
    outs = pl.pallas_call(
        _gate_up_kernel,
        grid=(m // tm, nj),
        in_specs=[_resident_rows(tm, k), _weight_cols(w, layer, tn), _weight_cols(w, layer, tn, nj), _row_stat(tm),
                  pl.BlockSpec((None, slab, n), lambda i, j: (layer, i * nj + j, 0))],
        out_specs=[pl.BlockSpec((tm, tn), lambda i, j: (i, j)),
                   pl.BlockSpec((slab, n), lambda i, j: (i * nj + j, 0))],
        out_shape=[jax.ShapeDtypeStruct((m, f), BF16), jax.ShapeDtypeStruct((f, n), BF16)],
        compiler_params=_params("arbitrary", "arbitrary"),
        name="gate_up",
    )(a, w, w, rnorm, w_down)
    return outs


def _mixer_out(dest, d):
    assert dest.shape[1] == d and dest.dtype == BF16
    return jax.ShapeDtypeStruct(dest.shape, BF16), pl.BlockSpec(memory_space=pl.ANY)


def _hgrn_sum_matrix(c):
    r = np.arange(c)[:, None]
    u = np.arange(c)[None, :]
    blocks = []
    n = c // 2
    while n >= 1:
        m = (r // (2 * n)) * (2 * n) + n - 1
        blocks.append(np.where((r & n) != 0, (u > m) & (u <= r), (u > r) & (u <= m)))
        n //= 2
    blocks.append(u <= r)
    blocks.append(u > r)
    p = np.concatenate(blocks, axis=0).astype(np.float32)
    return np.concatenate([p, p, p, np.zeros_like(p)], axis=1)


def _hgrn_block(qz, fz, iz, gz, lb, gn, p, st, c, n_sub):
    rows_all = n_sub * c
    n_low = c.bit_length() - 1
    q = qz * jax.nn.sigmoid(qz)
    forget = lb + (1.0 - lb) * jax.nn.sigmoid(fz)
    log_f = jnp.log(forget)
    k = 1.0 - forget
    v16 = iz.astype(BF16)
    gate = gz * jax.nn.sigmoid(gz)

    hi = log_f.astype(BF16)
    rem = log_f - hi.astype(F32)
    mid = rem.astype(BF16)
    lo = (rem - mid.astype(F32)).astype(BF16)
    zero = jnp.zeros((c, HEAD), BF16)
    pieces = [jnp.concatenate([hi[x * c:(x + 1) * c], mid[x * c:(x + 1) * c], lo[x * c:(x + 1) * c], zero],
                              axis=0) for x in range(n_sub)]
    decays = []
    for x in range(0, n_sub - 1, 2):
        pair = jnp.exp(jnp.dot(p, jnp.concatenate(pieces[x:x + 2], axis=1), preferred_element_type=F32))
        decays += [pair[:, :HEAD], pair[:, HEAD:]]
    if n_sub % 2:
        decays.append(jnp.exp(jnp.dot(p, pieces[-1], preferred_element_type=F32)))

    def stack(per_sub):
        return per_sub[0] if n_sub == 1 else jnp.concatenate(per_sub, axis=0)

    def prod(factors):
        out = None
        for f in factors:
            out = f if out is None else out * f
        return out

    def scaled(block, factors):
        f = prod(factors)
        return block if f is None else block * f

    cum_in = [dx[n_low * c:(n_low + 1) * c] for dx in decays]
    cum_out = [dx[(n_low + 1) * c:] for dx in decays]
    whole = [e[c - 1:c, :] for e in cum_in]
    level_decay = [stack([dx[lvl * c:(lvl + 1) * c] for dx in decays]) for lvl in reversed(range(n_low))]
    g = 1
    while g < n_sub:
        per_sub = []
        for x in range(n_sub):
            mid_sub = x - x % (2 * g) + g
            if x >= mid_sub:
                per_sub.append(scaled(cum_in[x], whole[mid_sub:x]))
            else:
                per_sub.append(scaled(cum_out[x], whole[x + 1:mid_sub]))
        level_decay.append(stack(per_sub))
        g *= 2
    e_in = stack([scaled(cum_in[x], whole[:x]) for x in range(n_sub)])
    e_out = stack([scaled(cum_out[x], whole[x + 1:]) for x in range(n_sub)])

    t_idx = lax.broadcasted_iota(jnp.int32, (rows_all, rows_all), 0)
    s_idx = lax.broadcasted_iota(jnp.int32, (rows_all, rows_all), 1)
    x_idx = t_idx ^ s_idx
    scores = jnp.zeros((rows_all, rows_all), F32)
    for lvl, e in enumerate(level_decay):
        s_l = lax.dot_general((q * e).astype(BF16), (k * e).astype(BF16), _NT, preferred_element_type=F32)
        scores = jnp.where(x_idx >= (1 << lvl), s_l, scores)
    diag = jnp.where(t_idx == s_idx, jnp.sum(q * k, axis=-1, keepdims=True), 0.0)
    scores = jnp.where(t_idx > s_idx, scores, diag)

    o = jnp.dot(scores.astype(BF16), v16, preferred_element_type=F32)
    o = o + lax.dot_general((q * e_in).astype(BF16), st.astype(BF16), _NT, preferred_element_type=F32)
    st = st * prod(whole) + lax.dot_general(v16, (k * e_out).astype(BF16), _TN, preferred_element_type=F32)
    o = o * lax.rsqrt(jnp.mean(o * o, axis=-1, keepdims=True) + EPS) * gn
    return o * gate, st


def _hgrn_kernel(*refs, layer, chunk, n_sub, groups_per_iter, n_iters, heads_per_step):
    q_ref, f_ref, i_ref, g_ref, lbl_ref, gn_ref, p_ref, s0_ref = refs[:8]
    o_ref, s_ref, st_ref = refs[-3:]
    step = pl.program_id(2)

    @pl.when(step == 0)
    def _():
        for hh in range(heads_per_step):
            st_ref[hh] = s0_ref[0, hh].T

    logits = lbl_ref[...]
    ex = jnp.exp(logits - jnp.max(logits, axis=0, keepdims=True))
    sm = ex / jnp.sum(ex, axis=0, keepdims=True)
    lb = jnp.zeros_like(sm[0:1])
    for r in range(1, layer + 1):
        lb = lb + sm[r:r + 1]
    gn = gn_ref[...]
    p = p_ref[...]
    group = n_sub * chunk

    def body(it, carry):
        for hh in range(heads_per_step):
            lanes = slice(hh * HEAD, (hh + 1) * HEAD)
            st = st_ref[hh]
            for g in range(groups_per_iter):
                start = (it * groups_per_iter + g) * group
                rows = pl.ds(start if isinstance(start, int) else pl.multiple_of(start, group), group)
                o, st = _hgrn_block(q_ref[rows, lanes], f_ref[rows, lanes], i_ref[rows, lanes],
                                    g_ref[rows, lanes], lb[:, lanes], gn[:, lanes], p, st, chunk, n_sub)
                o_ref[rows, lanes] = o.astype(o_ref.dtype)
            st_ref[hh] = st
        return carry

    if n_iters == 1:
        body(0, 0)
    else:
        lax.fori_loop(0, n_iters, body, 0)

    @pl.when(step == pl.num_programs(2) - 1)
    def _():
        for hh in range(heads_per_step):
            s_ref[0, hh] = st_ref[hh].T.astype(s_ref.dtype)


def hgrn_mixer(z, lb_logits, gnorm, s0, layer, row0, seq, chunk, rows_per_step, n_sub, groups_per_iter,
               heads_per_step, dest):
    d = z.shape[1] // 4
    width = heads_per_step * HEAD
    groups = d // width
    batch = s0.shape[0]
    steps = seq // rows_per_step
    blk0 = row0 // rows_per_step
    rows_per_iter = chunk * n_sub * groups_per_iter
    assert row0 % rows_per_step == 0 and seq % rows_per_step == 0 and rows_per_step % rows_per_iter == 0
    p = jnp.asarray(_hgrn_sum_matrix(chunk), BF16)

    def col(part):
        return pl.BlockSpec((rows_per_step, width), lambda b, h, s: (blk0 + b * steps + s, part * groups + h))

    head_row = lambda b, h, s: (0, h)
    state_blk = pl.BlockSpec((1, heads_per_step, HEAD, HEAD), lambda b, h, s: (b, h, 0, 0))
    out_shape, dest_spec = _mixer_out(dest, d)
    return pl.pallas_call(
        functools.partial(_hgrn_kernel, layer=layer, chunk=chunk, n_sub=n_sub, groups_per_iter=groups_per_iter,
                          n_iters=rows_per_step // rows_per_iter, heads_per_step=heads_per_step),
        grid=(batch, groups, steps),
        in_specs=[col(0), col(1), col(2), col(3),
                  pl.BlockSpec((lb_logits.shape[0], width), head_row),
                  pl.BlockSpec((1, width), head_row),
                  pl.BlockSpec(p.shape, lambda b, h, s: (0, 0)),
                  state_blk, dest_spec],
        out_specs=[pl.BlockSpec((rows_per_step, width), lambda b, h, s: (blk0 + b * steps + s, h)), state_blk],
        out_shape=[out_shape, jax.ShapeDtypeStruct(s0.shape, F32)],
        input_output_aliases={8: 0},
        scratch_shapes=[pltpu.VMEM((heads_per_step, HEAD, HEAD), F32)],
        compiler_params=_params("arbitrary", "arbitrary", "arbitrary"),
        name="hgrn_mixer",
    )(z, z, z, z, lb_logits, gnorm.reshape(1, d), p, s0, dest)


def _later_keys_matrix(tk):
    j = lax.broadcasted_iota(jnp.int32, (tk, tk), 0)
    s = lax.broadcasted_iota(jnp.int32, (tk, tk), 1)
    return jnp.where(j > s, 1.0, 0.0).astype(BF16)


def _sb_weights(z, later, run, key_offset, rows_per_head=None):
    z = z.astype(BF16)
    log_beta = jnp.minimum(z, 0) - jnp.log(1 + jnp.exp(-jnp.abs(z)))
    log_keep = log_beta - z
    if key_offset is not None:
        t_idx = lax.broadcasted_iota(jnp.int32, z.shape, 0)
        if rows_per_head is not None:
            t_idx = lax.rem(t_idx, rows_per_head)
        mask = lax.broadcasted_iota(jnp.int32, z.shape, 1) + key_offset < t_idx
        log_keep = jnp.where(mask, log_keep, jnp.zeros_like(log_keep))
    after = jnp.dot(log_keep, later, preferred_element_type=F32)
    w = jnp.exp(log_beta.astype(F32) + after + run)
    if key_offset is not None:
        w = jnp.where(mask, w, 0.0)
    return w.astype(BF16), run + jnp.sum(log_keep.astype(F32), axis=-1, keepdims=True)


def _sb_tile(q, ks, vs, later, acc, run, key_offset):
    w, run = _sb_weights(lax.dot_general(q, ks, _NT, preferred_element_type=F32), later, run, key_offset)
    return acc + jnp.dot(w, vs, preferred_element_type=F32), run


def _sb_prompt_kernel(q_ref, k_ref, v_ref, dest_ref, o_ref, *, tq, tk):
    del dest_ref
    qi = pl.program_id(2)
    q = q_ref[...]
    later = _later_keys_matrix(tk)
    per_q = tq // tk

    def visit(j, carry, key_offset):
        rows = pl.ds(j * tk if isinstance(j, int) else pl.multiple_of(j * tk, tk), tk)
        return _sb_tile(q, k_ref[rows, :], v_ref[rows, :], later, carry[0], carry[1], key_offset)

    carry = (jnp.zeros((tq, HEAD), F32), jnp.zeros((tq, 1), F32))
    for dgl in reversed(range(per_q)):
        r0 = dgl * tk
        rows = pl.ds(pl.multiple_of((qi * per_q + dgl) * tk, tk), tk)
        acc, run = _sb_tile(q[r0:], k_ref[rows, :], v_ref[rows, :], later, carry[0][r0:], carry[1][r0:], 0)
        if r0:
            acc = jnp.concatenate([carry[0][:r0], acc], axis=0)
            run = jnp.concatenate([carry[1][:r0], run], axis=0)
        carry = (acc, run)

    def quad(t, c):
        j = qi * per_q - 1 - 4 * t
        for u in range(4):
            c = visit(j - u, c, None)
        return c

    n_left = qi * per_q
    carry = lax.fori_loop(0, n_left // 4, quad, carry)
    carry = lax.cond(n_left % 4 != 0, lambda c: visit(0, visit(1, c, None), None), lambda c: c, carry)
    o_ref[...] = carry[0].astype(o_ref.dtype)


def sb_prompt(q, kv16, batch, seq, dest, tq=SB_QUERY_TILE, tk=SB_KEY_TILE):
    d = q.shape[1]
    heads = d // HEAD
    nq = seq // tq
    assert seq % tq == 0 and tq == 2 * tk
    out_shape, dest_spec = _mixer_out(dest, d)
    return pl.pallas_call(
        functools.partial(_sb_prompt_kernel, tq=tq, tk=tk),
        grid=(batch, heads, nq),
        in_specs=[pl.BlockSpec((tq, HEAD), lambda b, h, i: (b * nq + i, h)),
                  pl.BlockSpec((seq, HEAD), lambda b, h, i: (b, h)),
                  pl.BlockSpec((seq, HEAD), lambda b, h, i: (b, heads + h)),
                  dest_spec],
        out_specs=pl.BlockSpec((tq, HEAD), lambda b, h, i: (b * nq + i, h)),
        out_shape=out_shape,
        input_output_aliases={3: 0},
        compiler_params=_params("arbitrary", "arbitrary", "arbitrary"),
        name="sb_prompt",
    )(q, kv16, kv16, dest)


def _sb_sample_kernel(q_ref, k_ref, v_ref, ck_ref, cv_ref, dest_ref, o_ref, *, tk, heads_per_step):
    del dest_ref
    t = q_ref.shape[0]
    past = ck_ref.shape[0]
    lanes = [slice(hh * HEAD, (hh + 1) * HEAD) for hh in range(heads_per_step)]
    qs = [q_ref[:, ln] for ln in lanes]

    def visit(k_tile, v_tile, later, accs, run, key_offset):
        z = jnp.concatenate([lax.dot_general(q, k_tile(ln), _NT, preferred_element_type=F32)
                             for q, ln in zip(qs, lanes)], axis=0)
        w, run = _sb_weights(z, later, run, key_offset, t)
        accs = [acc + jnp.dot(w[hh * t:(hh + 1) * t], v_tile(ln), preferred_element_type=F32)
                for hh, (acc, ln) in enumerate(zip(accs, lanes))]
        return accs, run

    accs = [jnp.zeros((t, HEAD), F32) for _ in lanes]
    run = jnp.zeros((heads_per_step * t, 1), F32)
    accs, run = visit(lambda ln: k_ref[:, ln], lambda ln: v_ref[:, ln], _later_keys_matrix(t), accs, run, 0)
    later = _later_keys_matrix(tk)
    for j in range(past // tk - 1, -1, -1):
        rows = slice(j * tk, (j + 1) * tk)
        accs, run = visit(lambda ln: ck_ref[rows, ln].astype(BF16), lambda ln: cv_ref[rows, ln].astype(BF16),
                          later, accs, run, None)
    for acc, ln in zip(accs, lanes):
        o_ref[:, ln] = acc.astype(o_ref.dtype)


def sb_sample(q, kv16, cache_k, cache_v, row0, batch, seq, dest, tk=SB_KEY_TILE,
              heads_per_step=SAMPLE_HEADS_PER_STEP):
    d = q.shape[1]
    width = heads_per_step * HEAD
    groups = d // width
    past = cache_k.shape[1]
    blk0 = row0 // seq
    assert row0 % seq == 0 and past % tk == 0
    own = lambda off: pl.BlockSpec((seq, width), lambda b, h: (blk0 + b, off + h))
    cache_blk = pl.BlockSpec((None, past, width), lambda b, h: (b, 0, h))
    out_shape, dest_spec = _mixer_out(dest, d)
    return pl.pallas_call(
        functools.partial(_sb_sample_kernel, tk=tk, heads_per_step=heads_per_step),
        grid=(batch, groups),
        in_specs=[own(0), own(0), own(groups), cache_blk, cache_blk, dest_spec],
        out_specs=own(0),
        out_shape=out_shape,
        input_output_aliases={5: 0},
        compiler_params=_params("arbitrary", "arbitrary"),
        name="sb_sample",
    )(q, kv16, kv16, cache_k, cache_v, dest)


def kernel(x_prompt, x_sample, state_hgrn, cache_k, cache_v, norm_mix, norm_ffn, norm_kv, norm_out,
           a_w_in, a_lb_logits, a_gnorm, a_w_out, b_w_kv, b_w_q, b_w_out, ffn_w_gate_up, ffn_w_down):
    pb, ps, d = x_prompt.shape
    sb, ss, _ = x_sample.shape
    n_a = a_w_in.shape[0]
    depth = norm_mix.shape[0]
    heads = d // HEAD
    mp, ms = pb * ps, sb * ss
    m = mp + ms

    x = jnp.concatenate([x_prompt.reshape(mp, d), x_sample.reshape(ms, d)], axis=0)
    zero_state = jnp.zeros((pb, heads, HEAD, HEAD), F32)
    states_p, states_s = [], []
    kv32 = kv16 = ck32 = cv32 = h_kv = None
    o = jnp.zeros((m, d), BF16)
    h, rnorm = rmsnorm(x, norm_mix[0], BF16, 384), None
    for layer in range(depth):
        if layer < n_a:
            if layer == 0:
                z, ck32, cv32 = matmul_with_cache_gather(h, a_w_in, layer, cache_k, cache_v)
            else:
                z = matmul(h, a_w_in, layer, rnorm, tn=512)
            o, s_p = hgrn_mixer(z, a_lb_logits, a_gnorm[layer], zero_state, layer, 0, ps,
                                HGRN_CHUNK, 64 * HGRN_CHUNK, 4, 16, 1, o)
            o, s_s = hgrn_mixer(z, a_lb_logits, a_gnorm[layer], state_hgrn[layer], layer, mp, ss,
                                ss, ss, 1, 1, SAMPLE_HEADS_PER_STEP, o)
            states_p.append(s_p)
            states_s.append(s_s)
            x, (h_ffn,), rnorm_ffn = matmul(o, a_w_out, layer, res=x, gains=(norm_ffn[layer],))
        else:
            j = layer - n_a
            if j == 0:
                kv32, kv16 = matmul(h_kv, b_w_kv, None, rnorm, out_dtypes=(F32, BF16), tn=512)
            q = matmul(h, b_w_q, j, rnorm, out_dtypes=(BF16,), out_scale=SB_SCALE, tn=512)
            o = sb_prompt(q, kv16, pb, ps, o)
            o = sb_sample(q, kv16, ck32, cv32, mp, sb, ss, o)
            x, (h_ffn,), rnorm_ffn = matmul(o, b_w_out, j, res=x, gains=(norm_ffn[layer],))
        hidden, w_down16 = gate_up(h_ffn, ffn_w_gate_up, ffn_w_down, layer, rnorm_ffn)
        if layer + 1 == depth:
            x = matmul(hidden, w_down16, res=x, tm=1056)
        elif layer + 1 == n_a:
            x, (h, h_kv), rnorm = matmul(hidden, w_down16, res=x, gains=(norm_mix[layer + 1], norm_kv), tm=1056)
        else:
            x, (h,), rnorm = matmul(hidden, w_down16, res=x, gains=(norm_mix[layer + 1],), tm=1056)
    y_p = rmsnorm(x, norm_out, F32, 256, 0, mp)
    y_s = rmsnorm(x, norm_out, F32, 256, mp, ms)

    k_new, v_new = kv32[:, :d], kv32[:, d:]
    return (y_p.reshape(pb, ps, d), y_s.reshape(sb, ss, d),
            jnp.stack(states_p).astype(x.dtype), k_new[:mp].reshape(pb, ps, heads, HEAD),
            v_new[:mp].reshape(pb, ps, heads, HEAD),
            jnp.stack(states_s).astype(x.dtype), k_new[mp:].reshape(sb, ss, heads, HEAD),
            v_new[mp:].reshape(sb, ss, heads, HEAD))
```

```python
import functools

import numpy as np
import jax
import jax.numpy as jnp
from jax import lax
from jax.experimental import pallas as pl
from jax.experimental.pallas import tpu as pltpu

F32 = jnp.float32
BF16 = jnp.bfloat16
EPS = 1e-6
HEAD = 128
HGRN_CHUNK = 64
SB_KEY_TILE = 256
SB_QUERY_TILE = 2 * SB_KEY_TILE
SB_SCALE = HEAD ** -0.5
SAMPLE_HEADS_PER_STEP = 8
LANES = 128
V7X_VMEM_LIMIT_BYTES = 58 * 1024 * 1024

_NT = (((1,), (1,)), ((), ()))
_TN = (((0,), (0,)), ((), ()))


def _params(*sem):
    return pltpu.CompilerParams(dimension_semantics=sem, vmem_limit_bytes=V7X_VMEM_LIMIT_BYTES)


def _rmsnorm_kernel(x_ref, g_ref, o_ref):
    x = x_ref[...]
    y = x * lax.rsqrt(jnp.mean(x * x, axis=-1, keepdims=True) + EPS)
    o_ref[...] = (y * g_ref[...]).astype(o_ref.dtype)


def rmsnorm(x, g, out_dtype, rows, row0=0, nrows=None):
    d = x.shape[1]
    nrows = x.shape[0] if nrows is None else nrows
    assert nrows % rows == 0 and row0 % rows == 0
    blk0 = row0 // rows
    return pl.pallas_call(
        _rmsnorm_kernel,
        grid=(nrows // rows,),
        in_specs=[pl.BlockSpec((rows, d), lambda i: (blk0 + i, 0)), pl.BlockSpec((1, d), lambda i: (0, 0))],
        out_specs=pl.BlockSpec((rows, d), lambda i: (i, 0)),
        out_shape=jax.ShapeDtypeStruct((nrows, d), out_dtype),
        compiler_params=_params("arbitrary"),
        name="rmsnorm",
    )(x, g.reshape(1, d))


def _mxu_operand(w_ref):
    w = w_ref[...]
    return w if w.dtype == BF16 else w.astype(BF16)


def _row_scaled(acc, rnorm_ref):
    return acc * jnp.tile(rnorm_ref[...], (1, acc.shape[1] // LANES))


def _matmul_kernel(*refs, has_rnorm, has_res, n_out, n_gain, out_scale, out_width):
    refs = list(refs)
    a_ref, w_ref = refs.pop(0), refs.pop(0)
    rnorm_ref = refs.pop(0) if has_rnorm else None
    res_ref = refs.pop(0) if has_res else None
    gain_refs = [refs.pop(0) for _ in range(n_gain)]
    out_refs = [refs.pop(0) for _ in range(n_out)]
    normed_refs = [refs.pop(0) for _ in range(n_gain)]
    acc = jnp.dot(a_ref[...], _mxu_operand(w_ref), preferred_element_type=F32)
    if has_rnorm:
        acc = _row_scaled(acc, rnorm_ref)
    if out_scale is not None:
        acc = acc * out_scale
    if has_res:
        acc = res_ref[...] + acc
    for o_ref in out_refs:
        o_ref[...] = acc.astype(o_ref.dtype)
    if n_gain:
        stat_ref = refs.pop(0)
        j = pl.program_id(1)
        for g_ref, o_ref in zip(gain_refs, normed_refs):
            o_ref[...] = (acc * g_ref[...]).astype(o_ref.dtype)
        sq = acc * acc
        part = sq[:, :LANES]
        for c0 in range(LANES, sq.shape[1], LANES):
            part = part + sq[:, c0:c0 + LANES]

        @pl.when(j == 0)
        def _():
            stat_ref[...] = part

        @pl.when(j != 0)
        def _():
            stat_ref[...] = stat_ref[...] + part

        @pl.when(j == pl.num_programs(1) - 1)
        def _():
            mean = jnp.sum(stat_ref[...], axis=-1, keepdims=True) / out_width
            stat_ref[...] = jnp.broadcast_to(lax.rsqrt(mean + EPS), stat_ref.shape)


def _resident_rows(tm, k):
    return pl.BlockSpec((tm, k), lambda i, j: (i, 0), pipeline_mode=pl.Buffered(1))


def _weight_cols(w, layer, tn, col0=0):
    if w.ndim == 2:
        return pl.BlockSpec((w.shape[0], tn), lambda i, j: (0, j + col0))
    return pl.BlockSpec((None, w.shape[1], tn), lambda i, j: (layer, 0, j + col0))


def _row_stat(tm):
    return pl.BlockSpec((tm, LANES), lambda i, j: (i, 0))


def matmul(a, w, layer=None, rnorm=None, res=None, out_dtypes=(F32,), out_scale=None, gains=(), tm=2112, tn=256):
    m, k = a.shape
    n = w.shape[-1]
    assert m % tm == 0 and n % tn == 0 and tn % LANES == 0 and w.shape[-2] == k
    tile = pl.BlockSpec((tm, tn), lambda i, j: (i, j))
    in_specs = [_resident_rows(tm, k), _weight_cols(w, layer, tn)]
    args = [a, w]
    if rnorm is not None:
        in_specs.append(_row_stat(tm))
        args.append(rnorm)
    if res is not None:
        in_specs.append(tile)
        args.append(res)
    for g in gains:
        in_specs.append(pl.BlockSpec((1, tn), lambda i, j: (0, j)))
        args.append(g.reshape(1, n))
    out_specs = [tile for _ in out_dtypes] + [tile for _ in gains]
    out_shape = ([jax.ShapeDtypeStruct((m, n), dt) for dt in out_dtypes]
                 + [jax.ShapeDtypeStruct((m, n), BF16) for _ in gains])
    if gains:
        out_specs.append(_row_stat(tm))
        out_shape.append(jax.ShapeDtypeStruct((m, LANES), F32))
    outs = pl.pallas_call(
        functools.partial(_matmul_kernel, has_rnorm=rnorm is not None, has_res=res is not None,
                          n_out=len(out_dtypes), n_gain=len(gains), out_scale=out_scale, out_width=n),
        grid=(m // tm, n // tn),
        in_specs=in_specs,
        out_specs=out_specs,
        out_shape=out_shape,
        compiler_params=_params("arbitrary", "arbitrary"),
        name="matmul",
    )(*args)
    main = outs[0] if len(out_dtypes) == 1 else outs[:len(out_dtypes)]
    if not gains:
        return main
    return main, outs[len(out_dtypes):-1], outs[-1]


def _gate_up_kernel(a_ref, wg_ref, wu_ref, rnorm_ref, wd_ref, o_ref, wd16_ref):
    a = a_ref[...]
    gate = _row_scaled(jnp.dot(a, _mxu_operand(wg_ref), preferred_element_type=F32), rnorm_ref)
    up = _row_scaled(jnp.dot(a, _mxu_operand(wu_ref), preferred_element_type=F32), rnorm_ref)
    o_ref[...] = (gate * jax.nn.sigmoid(gate) * up).astype(o_ref.dtype)
    wd16_ref[...] = wd_ref[...].astype(wd16_ref.dtype)


def gate_up(a, w, w_down, layer, rnorm, tm=2112, tn=256):
    m, k = a.shape
    f = w.shape[-1] // 2
    n = w_down.shape[-1]
    nj = f // tn
    steps = (m // tm) * nj
    slab = f // steps
    assert m % tm == 0 and f % tn == 0 and tn % LANES == 0 and w.shape[-2] == k
    assert w_down.shape[-2] == f and f % steps == 0 and slab % 16 == 0
    outs = pl.pallas_call(
        _gate_up_kernel,
        grid=(m // tm, nj),
        in_specs=[_resident_rows(tm, k), _weight_cols(w, layer, tn), _weight_cols(w, layer, tn, nj), _row_stat(tm),
                  pl.BlockSpec((None, slab, n), lambda i, j: (layer, i * nj + j, 0))],
        out_specs=[pl.BlockSpec((tm, tn), lambda i, j: (i, j)),
                   pl.BlockSpec((slab, n), lambda i, j: (i * nj + j, 0))],
        out_shape=[jax.ShapeDtypeStruct((m, f), BF16), jax.ShapeDtypeStruct((f, n), BF16)],
        compiler_params=_params("arbitrary", "arbitrary"),
        name="gate_up",
    )(a, w, w, rnorm, w_down)
    return outs


def _mixer_out(dest, d):
    assert dest.shape[1] == d and dest.dtype == BF16
    return jax.ShapeDtypeStruct(dest.shape, BF16), pl.BlockSpec(memory_space=pl.ANY)


def _hgrn_sum_matrix(c):
    r = np.arange(c)[:, None]
    u = np.arange(c)[None, :]
    blocks = []
    n = c // 2
    while n >= 1:
        m = (r // (2 * n)) * (2 * n) + n - 1
        blocks.append(np.where((r & n) != 0, (u > m) & (u <= r), (u > r) & (u <= m)))
        n //= 2
    blocks.append(u <= r)
    blocks.append(u > r)
    p = np.concatenate(blocks, axis=0).astype(np.float32)
    return np.concatenate([p, p, p, np.zeros_like(p)], axis=1)


def _hgrn_block(qz, fz, iz, gz, lb, gn, p, st, c, n_sub):
    rows_all = n_sub * c
    n_low = c.bit_length() - 1
    q = qz * jax.nn.sigmoid(qz)
    forget = lb + (1.0 - lb) * jax.nn.sigmoid(fz)
    log_f = jnp.log(forget)
    k = 1.0 - forget
    v16 = iz.astype(BF16)
    gate = gz * jax.nn.sigmoid(gz)

    hi = log_f.astype(BF16)
    rem = log_f - hi.astype(F32)
    mid = rem.astype(BF16)
    lo = (rem - mid.astype(F32)).astype(BF16)
    zero = jnp.zeros((c, HEAD), BF16)
    pieces = [jnp.concatenate([hi[x * c:(x + 1) * c], mid[x * c:(x + 1) * c], lo[x * c:(x + 1) * c], zero],
                              axis=0) for x in range(n_sub)]
    decays = []
    for x in range(0, n_sub - 1, 2):
        pair = jnp.exp(jnp.dot(p, jnp.concatenate(pieces[x:x + 2], axis=1), preferred_element_type=F32))
        decays += [pair[:, :HEAD], pair[:, HEAD:]]
    if n_sub % 2:
        decays.append(jnp.exp(jnp.dot(p, pieces[-1], preferred_element_type=F32)))

    def stack(per_sub):
        return per_sub[0] if n_sub == 1 else jnp.concatenate(per_sub, axis=0)

    def prod(factors):
        out = None
        for f in factors:
            out = f if out is None else out * f
        return out

    def scaled(block, factors):
        f = prod(factors)
        return block if f is None else block * f

    cum_in = [dx[n_low * c:(n_low + 1) * c] for dx in decays]
    cum_out = [dx[(n_low + 1) * c:] for dx in decays]
    whole = [e[c - 1:c, :] for e in cum_in]
    level_decay = [stack([dx[lvl * c:(lvl + 1) * c] for dx in decays]) for lvl in reversed(range(n_low))]
    g = 1
    while g < n_sub:
        per_sub = []
        for x in range(n_sub):
            mid_sub = x - x % (2 * g) + g
            if x >= mid_sub:
                per_sub.append(scaled(cum_in[x], whole[mid_sub:x]))
            else:
                per_sub.append(scaled(cum_out[x], whole[x + 1:mid_sub]))
        level_decay.append(stack(per_sub))
        g *= 2
    e_in = stack([scaled(cum_in[x], whole[:x]) for x in range(n_sub)])
    e_out = stack([scaled(cum_out[x], whole[x + 1:]) for x in range(n_sub)])

    t_idx = lax.broadcasted_iota(jnp.int32, (rows_all, rows_all), 0)
    s_idx = lax.broadcasted_iota(jnp.int32, (rows_all, rows_all), 1)
    x_idx = t_idx ^ s_idx
    scores = jnp.zeros((rows_all, rows_all), F32)
    for lvl, e in enumerate(level_decay):
        s_l = lax.dot_general((q * e).astype(BF16), (k * e).astype(BF16), _NT, preferred_element_type=F32)
        scores = jnp.where(x_idx >= (1 << lvl), s_l, scores)
    diag = jnp.where(t_idx == s_idx, jnp.sum(q * k, axis=-1, keepdims=True), 0.0)
    scores = jnp.where(t_idx > s_idx, scores, diag)

    o = jnp.dot(scores.astype(BF16), v16, preferred_element_type=F32)
    o = o + lax.dot_general((q * e_in).astype(BF16), st.astype(BF16), _NT, preferred_element_type=F32)
    st = st * prod(whole) + lax.dot_general(v16, (k * e_out).astype(BF16), _TN, preferred_element_type=F32)
    o = o * lax.rsqrt(jnp.mean(o * o, axis=-1, keepdims=True) + EPS) * gn
    return o * gate, st


def _hgrn_kernel(*refs, layer, chunk, n_sub, groups_per_iter, n_iters, heads_per_step):
    q_ref, f_ref, i_ref, g_ref, lbl_ref, gn_ref, p_ref, s0_ref = refs[:8]
    o_ref, s_ref, st_ref = refs[-3:]
    step = pl.program_id(2)

    @pl.when(step == 0)
    def _():
        for hh in range(heads_per_step):
            st_ref[hh] = s0_ref[0, hh].T

    logits = lbl_ref[...]
    ex = jnp.exp(logits - jnp.max(logits, axis=0, keepdims=True))
    sm = ex / jnp.sum(ex, axis=0, keepdims=True)
    lb = jnp.zeros_like(sm[0:1])
    for r in range(1, layer + 1):
        lb = lb + sm[r:r + 1]
    gn = gn_ref[...]
    p = p_ref[...]
    group = n_sub * chunk

    def body(it, carry):
        for hh in range(heads_per_step):
            lanes = slice(hh * HEAD, (hh + 1) * HEAD)
            st = st_ref[hh]
            for g in range(groups_per_iter):
                start = (it * groups_per_iter + g) * group
                rows = pl.ds(start if isinstance(start, int) else pl.multiple_of(start, group), group)
                o, st = _hgrn_block(q_ref[rows, lanes], f_ref[rows, lanes], i_ref[rows, lanes],
                                    g_ref[rows, lanes], lb[:, lanes], gn[:, lanes], p, st, chunk, n_sub)
                o_ref[rows, lanes] = o.astype(o_ref.dtype)
            st_ref[hh] = st
        return carry

    if n_iters == 1:
        body(0, 0)
    else:
        lax.fori_loop(0, n_iters, body, 0)

    @pl.when(step == pl.num_programs(2) - 1)
    def _():
        for hh in range(heads_per_step):
            s_ref[0, hh] = st_ref[hh].T.astype(s_ref.dtype)


def hgrn_mixer(z, lb_logits, gnorm, s0, layer, row0, seq, chunk, rows_per_step, n_sub, groups_per_iter,
               heads_per_step, dest):
    d = z.shape[1] // 4
    width = heads_per_step * HEAD
    groups = d // width
    batch = s0.shape[0]
    steps = seq // rows_per_step
    blk0 = row0 // rows_per_step
    rows_per_iter = chunk * n_sub * groups_per_iter
    assert row0 % rows_per_step == 0 and seq % rows_per_step == 0 and rows_per_step % rows_per_iter == 0
    p = jnp.asarray(_hgrn_sum_matrix(chunk), BF16)

    def col(part):
        return pl.BlockSpec((rows_per_step, width), lambda b, h, s: (blk0 + b * steps + s, part * groups + h))

    head_row = lambda b, h, s: (0, h)
    state_blk = pl.BlockSpec((1, heads_per_step, HEAD, HEAD), lambda b, h, s: (b, h, 0, 0))
    out_shape, dest_spec = _mixer_out(dest, d)
    return pl.pallas_call(
        functools.partial(_hgrn_kernel, layer=layer, chunk=chunk, n_sub=n_sub, groups_per_iter=groups_per_iter,
                          n_iters=rows_per_step // rows_per_iter, heads_per_step=heads_per_step),
        grid=(batch, groups, steps),
        in_specs=[col(0), col(1), col(2), col(3),
                  pl.BlockSpec((lb_logits.shape[0], width), head_row),
                  pl.BlockSpec((1, width), head_row),
                  pl.BlockSpec(p.shape, lambda b, h, s: (0, 0)),
                  state_blk, dest_spec],
        out_specs=[pl.BlockSpec((rows_per_step, width), lambda b, h, s: (blk0 + b * steps + s, h)), state_blk],
        out_shape=[out_shape, jax.ShapeDtypeStruct(s0.shape, F32)],
        input_output_aliases={8: 0},
        scratch_shapes=[pltpu.VMEM((heads_per_step, HEAD, HEAD), F32)],
        compiler_params=_params("arbitrary", "arbitrary", "arbitrary"),
        name="hgrn_mixer",
    )(z, z, z, z, lb_logits, gnorm.reshape(1, d), p, s0, dest)


def _later_keys_matrix(tk):
    j = lax.broadcasted_iota(jnp.int32, (tk, tk), 0)
    s = lax.broadcasted_iota(jnp.int32, (tk, tk), 1)
    return jnp.where(j > s, 1.0, 0.0).astype(BF16)


def _sb_weights(z, later, run, key_offset, rows_per_head=None):
    z = z.astype(BF16)
    log_beta = jnp.minimum(z, 0) - jnp.log(1 + jnp.exp(-jnp.abs(z)))
    log_keep = log_beta - z
    if key_offset is not None:
        t_idx = lax.broadcasted_iota(jnp.int32, z.shape, 0)
        if rows_per_head is not None:
            t_idx = lax.rem(t_idx, rows_per_head)
        mask = lax.broadcasted_iota(jnp.int32, z.shape, 1) + key_offset < t_idx
        log_keep = jnp.where(mask, log_keep, jnp.zeros_like(log_keep))
    after = jnp.dot(log_keep, later, preferred_element_type=F32)
    w = jnp.exp(log_beta.astype(F32) + after + run)
    if key_offset is not None:
        w = jnp.where(mask, w, 0.0)
    return w.astype(BF16), run + jnp.sum(log_keep.astype(F32), axis=-1, keepdims=True)


def _sb_tile(q, ks, vs, later, acc, run, key_offset):
    w, run = _sb_weights(lax.dot_general(q, ks, _NT, preferred_element_type=F32), later, run, key_offset)
    return acc + jnp.dot(w, vs, preferred_element_type=F32), run


def _sb_prompt_kernel(q_ref, k_ref, v_ref, dest_ref, o_ref, *, tq, tk, heads_per_step):
    del dest_ref
    qi = pl.program_id(2)
    later = _later_keys_matrix(tk)
    per_q = tq // tk
    lanes = [slice(hh * HEAD, (hh + 1) * HEAD) for hh in range(heads_per_step)]
    qs = [q_ref[:, ln] for ln in lanes]

    def visit(j, carry, key_offset):
        rows = pl.ds(j * tk if isinstance(j, int) else pl.multiple_of(j * tk, tk), tk)
        return tuple(_sb_tile(q, k_ref[rows, ln], v_ref[rows, ln], later, c[0], c[1], key_offset)
                     for q, ln, c in zip(qs, lanes, carry))

    carry = tuple((jnp.zeros((tq, HEAD), F32), jnp.zeros((tq, 1), F32)) for _ in lanes)
    for dgl in reversed(range(per_q)):
        r0 = dgl * tk
        rows = pl.ds(pl.multiple_of((qi * per_q + dgl) * tk, tk), tk)
        new = []
        for q, ln, c in zip(qs, lanes, carry):
            acc, run = _sb_tile(q[r0:], k_ref[rows, ln], v_ref[rows, ln], later, c[0][r0:], c[1][r0:], 0)
            if r0:
                acc = jnp.concatenate([c[0][:r0], acc], axis=0)
                run = jnp.concatenate([c[1][:r0], run], axis=0)
            new.append((acc, run))
        carry = tuple(new)

    def quad(t, c):
        j = qi * per_q - 1 - 4 * t
        for u in range(4):
            c = visit(j - u, c, None)
        return c

    n_left = qi * per_q
    carry = lax.fori_loop(0, n_left // 4, quad, carry)
    carry = lax.cond(n_left % 4 != 0, lambda c: visit(0, visit(1, c, None), None), lambda c: c, carry)
    for ln, c in zip(lanes, carry):
        o_ref[:, ln] = c[0].astype(o_ref.dtype)


def sb_prompt(q, kv16, batch, seq, dest, tq=SB_QUERY_TILE, tk=SB_KEY_TILE, heads_per_step=2):
    d = q.shape[1]
    width = heads_per_step * HEAD
    groups = d // width
    nq = seq // tq
    assert seq % tq == 0 and tq == 2 * tk and d % width == 0
    out_shape, dest_spec = _mixer_out(dest, d)
    return pl.pallas_call(
        functools.partial(_sb_prompt_kernel, tq=tq, tk=tk, heads_per_step=heads_per_step),
        grid=(batch, groups, nq),
        in_specs=[pl.BlockSpec((tq, width), lambda b, h, i: (b * nq + i, h)),
                  pl.BlockSpec((seq, width), lambda b, h, i: (b, h)),
                  pl.BlockSpec((seq, width), lambda b, h, i: (b, groups + h)),
                  dest_spec],
        out_specs=pl.BlockSpec((tq, width), lambda b, h, i: (b * nq + i, h)),
        out_shape=out_shape,
        input_output_aliases={3: 0},
        compiler_params=_params("arbitrary", "arbitrary", "arbitrary"),
        name="sb_prompt",
    )(q, kv16, kv16, dest)


def _cache_rows_kernel(ck_hbm, cv_hbm, ok_ref, ov_ref, buf, sem, *, heads_per_step, groups):
    n = pl.program_id(0) * groups + pl.program_id(1)
    steps = pl.num_programs(0) * groups

    def copies(step, slot):
        b, g = step // groups, step % groups
        return [pltpu.make_async_copy(src.at[b, :, g * heads_per_step + hh, :], buf.at[slot, a, hh],
                                      sem.at[slot, a, hh])
                for a, src in enumerate((ck_hbm, cv_hbm)) for hh in range(heads_per_step)]

    @pl.when(n == 0)
    def _():
        for c in copies(n, 0):
            c.start()

    @pl.when(n + 1 < steps)
    def _():
        for c in copies(n + 1, (n + 1) % 2):
            c.start()

    slot = n % 2
    for c in copies(n, slot):
        c.wait()
    for a, o_ref in enumerate((ok_ref, ov_ref)):
        for hh in range(heads_per_step):
            o_ref[:, hh * HEAD:(hh + 1) * HEAD] = buf[slot, a, hh].astype(o_ref.dtype)


def cache_rows(cache_k, cache_v, heads_per_step=SAMPLE_HEADS_PER_STEP):
    batch, past, heads, head = cache_k.shape
    assert head == HEAD and heads % heads_per_step == 0 and cache_v.shape == cache_k.shape
    groups = heads // heads_per_step
    out = jax.ShapeDtypeStruct((batch, past, heads * head), BF16)
    blk = pl.BlockSpec((None, past, heads_per_step * head), lambda b, g: (b, 0, g))
    return pl.pallas_call(
        functools.partial(_cache_rows_kernel, heads_per_step=heads_per_step, groups=groups),
        grid=(batch, groups),
        in_specs=[pl.BlockSpec(memory_space=pl.ANY), pl.BlockSpec(memory_space=pl.ANY)],
        out_specs=[blk, blk],
        out_shape=[out, out],
        scratch_shapes=[pltpu.VMEM((2, 2, heads_per_step, past, head), F32),
                        pltpu.SemaphoreType.DMA((2, 2, heads_per_step))],
        compiler_params=_params("arbitrary", "arbitrary"),
        name="cache_rows",
    )(cache_k, cache_v)


def _sb_sample_kernel(q_ref, k_ref, v_ref, ck_ref, cv_ref, dest_ref, o_ref, *, tk, heads_per_step):
    del dest_ref
    t = q_ref.shape[0]
    past = ck_ref.shape[0]
    lanes = [slice(hh * HEAD, (hh + 1) * HEAD) for hh in range(heads_per_step)]
    qs = [q_ref[:, ln] for ln in lanes]

    def visit(k_tile, v_tile, later, accs, run, key_offset):
        z = jnp.concatenate([lax.dot_general(q, k_tile(ln), _NT, preferred_element_type=F32)
                             for q, ln in zip(qs, lanes)], axis=0)
        w, run = _sb_weights(z, later, run, key_offset, t)
        accs = [acc + jnp.dot(w[hh * t:(hh + 1) * t], v_tile(ln), preferred_element_type=F32)
                for hh, (acc, ln) in enumerate(zip(accs, lanes))]
        return accs, run

    accs = [jnp.zeros((t, HEAD), F32) for _ in lanes]
    run = jnp.zeros((heads_per_step * t, 1), F32)
    accs, run = visit(lambda ln: k_ref[:, ln], lambda ln: v_ref[:, ln], _later_keys_matrix(t), accs, run, 0)
    later = _later_keys_matrix(tk)
    for j in range(past // tk - 1, -1, -1):
        rows = slice(j * tk, (j + 1) * tk)
        accs, run = visit(lambda ln: ck_ref[rows, ln], lambda ln: cv_ref[rows, ln], later, accs, run, None)
    for acc, ln in zip(accs, lanes):
        o_ref[:, ln] = acc.astype(o_ref.dtype)


def sb_sample(q, kv16, cache_k, cache_v, row0, batch, seq, dest, tk=SB_KEY_TILE,
              heads_per_step=SAMPLE_HEADS_PER_STEP):
    d = q.shape[1]
    width = heads_per_step * HEAD
    groups = d // width
    past = cache_k.shape[1]
    blk0 = row0 // seq
    assert row0 % seq == 0 and past % tk == 0
    own = lambda off: pl.BlockSpec((seq, width), lambda b, h: (blk0 + b, off + h))
    cache_blk = pl.BlockSpec((None, past, width), lambda b, h: (b, 0, h))
    out_shape, dest_spec = _mixer_out(dest, d)
    return pl.pallas_call(
        functools.partial(_sb_sample_kernel, tk=tk, heads_per_step=heads_per_step),
        grid=(batch, groups),
        in_specs=[own(0), own(0), own(groups), cache_blk, cache_blk, dest_spec],
        out_specs=own(0),
        out_shape=out_shape,
        input_output_aliases={5: 0},
        compiler_params=_params("arbitrary", "arbitrary"),
        name="sb_sample",
    )(q, kv16, kv16, cache_k, cache_v, dest)


def kernel(x_prompt, x_sample, state_hgrn, cache_k, cache_v, norm_mix, norm_ffn, norm_kv, norm_out,
           a_w_in, a_lb_logits, a_gnorm, a_w_out, b_w_kv, b_w_q, b_w_out, ffn_w_gate_up, ffn_w_down):
    pb, ps, d = x_prompt.shape
    sb, ss, _ = x_sample.shape
    n_a = a_w_in.shape[0]
    depth = norm_mix.shape[0]
    heads = d // HEAD
    mp, ms = pb * ps, sb * ss
    m = mp + ms

    x = jnp.concatenate([x_prompt.reshape(mp, d), x_sample.reshape(ms, d)], axis=0)
    zero_state = jnp.zeros((pb, heads, HEAD, HEAD), F32)
    states_p, states_s = [], []
    kv32 = kv16 = ck16 = cv16 = h_kv = None
    o = jnp.zeros((m, d), BF16)
    h, rnorm = rmsnorm(x, norm_mix[0], BF16, 384), None
    for layer in range(depth):
        if layer < n_a:
            z = matmul(h, a_w_in, layer, rnorm, tn=512)
            o, s_p = hgrn_mixer(z, a_lb_logits, a_gnorm[layer], zero_state, layer, 0, ps,
                                HGRN_CHUNK, 64 * HGRN_CHUNK, 4, 16, 1, o)
            o, s_s = hgrn_mixer(z, a_lb_logits, a_gnorm[layer], state_hgrn[layer], layer, mp, ss,
                                ss, ss, 1, 1, SAMPLE_HEADS_PER_STEP, o)
            states_p.append(s_p)
            states_s.append(s_s)
            x, (h_ffn,), rnorm_ffn = matmul(o, a_w_out, layer, res=x, gains=(norm_ffn[layer],))
        else:
            j = layer - n_a
            if j == 0:
                kv32, kv16 = matmul(h_kv, b_w_kv, None, rnorm, out_dtypes=(F32, BF16), tn=512)
                ck16, cv16 = cache_rows(cache_k, cache_v)
            q = matmul(h, b_w_q, j, rnorm, out_dtypes=(BF16,), out_scale=SB_SCALE, tn=512)
            o = sb_prompt(q, kv16, pb, ps, o)
            o = sb_sample(q, kv16, ck16, cv16, mp, sb, ss, o)
            x, (h_ffn,), rnorm_ffn = matmul(o, b_w_out, j, res=x, gains=(norm_ffn[layer],))
        hidden, w_down16 = gate_up(h_ffn, ffn_w_gate_up, ffn_w_down, layer, rnorm_ffn)
        if layer + 1 == depth:
            x = matmul(hidden, w_down16, res=x, tm=1056)
        elif layer + 1 == n_a:
            x, (h, h_kv), rnorm = matmul(hidden, w_down16, res=x, gains=(norm_mix[layer + 1], norm_kv), tm=1056)
        else:
            x, (h,), rnorm = matmul(hidden, w_down16, res=x, gains=(norm_mix[layer + 1],), tm=1056)
    y_p = rmsnorm(x, norm_out, F32, 256, 0, mp)
    y_s = rmsnorm(x, norm_out, F32, 256, mp, ms)

    k_new, v_new = kv32[:, :d], kv32[:, d:]
    return (y_p.reshape(pb, ps, d), y_s.reshape(sb, ss, d),
            jnp.stack(states_p).astype(x.dtype), k_new[:mp].reshape(pb, ps, heads, HEAD),
            v_new[:mp].reshape(pb, ps, heads, HEAD),
            jnp.stack(states_s).astype(x.dtype), k_new[mp:].reshape(sb, ss, heads, HEAD),
            v_new[mp:].reshape(sb, ss, heads, HEAD))
```

```python
import functools

import numpy as np
import jax
import jax.numpy as jnp
from jax import lax
from jax.experimental import pallas as pl
from jax.experimental.pallas import tpu as pltpu

F32 = jnp.float32
BF16 = jnp.bfloat16
EPS = 1e-6
HEAD = 128
HGRN_CHUNK = 64
SB_KEY_TILE = 256
SB_QUERY_TILE = 2 * SB_KEY_TILE
SB_SCALE = HEAD ** -0.5
SAMPLE_HEADS_PER_STEP = 8
LANES = 128
V7X_VMEM_LIMIT_BYTES = 58 * 1024 * 1024

_NT = (((1,), (1,)), ((), ()))
_TN = (((0,), (0,)), ((), ()))


def _params(*sem):
    return pltpu.CompilerParams(dimension_semantics=sem, vmem_limit_bytes=V7X_VMEM_LIMIT_BYTES)


def _rmsnorm_kernel(x_ref, g_ref, o_ref):
    x = x_ref[...]
    y = x * lax.rsqrt(jnp.mean(x * x, axis=-1, keepdims=True) + EPS)
    o_ref[...] = (y * g_ref[...]).astype(o_ref.dtype)


def rmsnorm(x, g, out_dtype, rows, row0=0, nrows=None):
    d = x.shape[1]
    nrows = x.shape[0] if nrows is None else nrows
    assert nrows % rows == 0 and row0 % rows == 0
    blk0 = row0 // rows
    return pl.pallas_call(
        _rmsnorm_kernel,
        grid=(nrows // rows,),
        in_specs=[pl.BlockSpec((rows, d), lambda i: (blk0 + i, 0)), pl.BlockSpec((1, d), lambda i: (0, 0))],
        out_specs=pl.BlockSpec((rows, d), lambda i: (i, 0)),
        out_shape=jax.ShapeDtypeStruct((nrows, d), out_dtype),
        compiler_params=_params("arbitrary"),
        name="rmsnorm",
    )(x, g.reshape(1, d))


def _mxu_operand(w_ref):
    w = w_ref[...]
    return w if w.dtype == BF16 else w.astype(BF16)


def _row_scaled(acc, rnorm_ref):
    return acc * jnp.tile(rnorm_ref[...], (1, acc.shape[1] // LANES))


def _matmul_kernel(*refs, has_rnorm, has_res, n_out, n_gain, out_scale, out_width):
    refs = list(refs)
    a_ref, w_ref = refs.pop(0), refs.pop(0)
    rnorm_ref = refs.pop(0) if has_rnorm else None
    res_ref = refs.pop(0) if has_res else None
    gain_refs = [refs.pop(0) for _ in range(n_gain)]
    out_refs = [refs.pop(0) for _ in range(n_out)]
    normed_refs = [refs.pop(0) for _ in range(n_gain)]
    acc = jnp.dot(a_ref[...], _mxu_operand(w_ref), preferred_element_type=F32)
    if has_rnorm:
        acc = _row_scaled(acc, rnorm_ref)
    if out_scale is not None:
        acc = acc * out_scale
    if has_res:
        acc = res_ref[...] + acc
    for o_ref in out_refs:
        o_ref[...] = acc.astype(o_ref.dtype)
    if n_gain:
        stat_ref = refs.pop(0)
        j = pl.program_id(1)
        for g_ref, o_ref in zip(gain_refs, normed_refs):
            o_ref[...] = (acc * g_ref[...]).astype(o_ref.dtype)
        sq = acc * acc
        part = sq[:, :LANES]
        for c0 in range(LANES, sq.shape[1], LANES):
            part = part + sq[:, c0:c0 + LANES]

        @pl.when(j == 0)
        def _():
            stat_ref[...] = part

        @pl.when(j != 0)
        def _():
            stat_ref[...] = stat_ref[...] + part

        @pl.when(j == pl.num_programs(1) - 1)
        def _():
            mean = jnp.sum(stat_ref[...], axis=-1, keepdims=True) / out_width
            stat_ref[...] = jnp.broadcast_to(lax.rsqrt(mean + EPS), stat_ref.shape)


def _resident_rows(tm, k):
    return pl.BlockSpec((tm, k), lambda i, j: (i, 0), pipeline_mode=pl.Buffered(1))


def _weight_cols(w, layer, tn, col0=0):
    if w.ndim == 2:
        return pl.BlockSpec((w.shape[0], tn), lambda i, j: (0, j + col0))
    return pl.BlockSpec((None, w.shape[1], tn), lambda i, j: (layer, 0, j + col0))


def _row_stat(tm):
    return pl.BlockSpec((tm, LANES), lambda i, j: (i, 0))


def matmul(a, w, layer=None, rnorm=None, res=None, out_dtypes=(F32,), out_scale=None, gains=(), tm=2112, tn=256):
    m, k = a.shape
    n = w.shape[-1]
    assert m % tm == 0 and n % tn == 0 and tn % LANES == 0 and w.shape[-2] == k
    tile = pl.BlockSpec((tm, tn), lambda i, j: (i, j))
    in_specs = [_resident_rows(tm, k), _weight_cols(w, layer, tn)]
    args = [a, w]
    if rnorm is not None:
        in_specs.append(_row_stat(tm))
        args.append(rnorm)
    if res is not None:
        in_specs.append(tile)
        args.append(res)
    for g in gains:
        in_specs.append(pl.BlockSpec((1, tn), lambda i, j: (0, j)))
        args.append(g.reshape(1, n))
    out_specs = [tile for _ in out_dtypes] + [tile for _ in gains]
    out_shape = ([jax.ShapeDtypeStruct((m, n), dt) for dt in out_dtypes]
                 + [jax.ShapeDtypeStruct((m, n), BF16) for _ in gains])
    if gains:
        out_specs.append(_row_stat(tm))
        out_shape.append(jax.ShapeDtypeStruct((m, LANES), F32))
    outs = pl.pallas_call(
        functools.partial(_matmul_kernel, has_rnorm=rnorm is not None, has_res=res is not None,
                          n_out=len(out_dtypes), n_gain=len(gains), out_scale=out_scale, out_width=n),
        grid=(m // tm, n // tn),
        in_specs=in_specs,
        out_specs=out_specs,
        out_shape=out_shape,
        compiler_params=_params("arbitrary", "arbitrary"),
        name="matmul",
    )(*args)
    main = outs[0] if len(out_dtypes) == 1 else outs[:len(out_dtypes)]
    if not gains:
        return main
    return main, outs[len(out_dtypes):-1], outs[-1]


def _gate_up_kernel(a_ref, wg_ref, wu_ref, rnorm_ref, wd_ref, o_ref, wd16_ref):
    a = a_ref[...]
    gate = _row_scaled(jnp.dot(a, _mxu_operand(wg_ref), preferred_element_type=F32), rnorm_ref)
    up = _row_scaled(jnp.dot(a, _mxu_operand(wu_ref), preferred_element_type=F32), rnorm_ref)
    o_ref[...] = (gate * jax.nn.sigmoid(gate) * up).astype(o_ref.dtype)
    wd16_ref[...] = wd_ref[...].astype(wd16_ref.dtype)


def gate_up(a, w, w_down, layer, rnorm, tm=2112, tn=256):
    m, k = a.shape
    f = w.shape[-1] // 2
    n = w_down.shape[-1]
    nj = f // tn
    steps = (m // tm) * nj
    slab = f // steps
    assert m % tm == 0 and f % tn == 0 and tn % LANES == 0 and w.shape[-2] == k
    assert w_down.shape[-2] == f and f % steps == 0 and slab % 16 == 0
    outs = pl.pallas_call(
        _gate_up_kernel,
        grid=(m // tm, nj),
        in_specs=[_resident_rows(tm, k), _weight_cols(w, layer, tn), _weight_cols(w, layer, tn, nj), _row_stat(tm),
                  pl.BlockSpec((None, slab, n), lambda i, j: (layer, i * nj + j, 0))],
        out_specs=[pl.BlockSpec((tm, tn), lambda i, j: (i, j)),
                   pl.BlockSpec((slab, n), lambda i, j: (i * nj + j, 0))],
        out_shape=[jax.ShapeDtypeStruct((m, f), BF16), jax.ShapeDtypeStruct((f, n), BF16)],
        compiler_params=_params("arbitrary", "arbitrary"),
        name="gate_up",
    )(a, w, w, rnorm, w_down)
    return outs


def _mixer_out(dest, d):
    assert dest.shape[1] == d and dest.dtype == BF16
    return jax.ShapeDtypeStruct(dest.shape, BF16), pl.BlockSpec(memory_space=pl.ANY)


def _hgrn_sum_matrix(c):
    r = np.arange(c)[:, None]
    u = np.arange(c)[None, :]
    blocks = []
    n = c // 2
    while n >= 1:
        m = (r // (2 * n)) * (2 * n) + n - 1
        blocks.append(np.where((r & n) != 0, (u > m) & (u <= r), (u > r) & (u <= m)))
        n //= 2
    blocks.append(u <= r)
    blocks.append(u > r)
    p = np.concatenate(blocks, axis=0).astype(np.float32)
    return np.concatenate([p, p, p, np.zeros_like(p)], axis=1)


def _hgrn_block(qz, fz, iz, gz, lb, gn, p, st, c, n_sub):
    rows_all = n_sub * c
    n_low = c.bit_length() - 1
    q = qz * jax.nn.sigmoid(qz)
    forget = lb + (1.0 - lb) * jax.nn.sigmoid(fz)
    log_f = jnp.log(forget)
    k = 1.0 - forget
    v16 = iz.astype(BF16)
    gate = gz * jax.nn.sigmoid(gz)

    hi = log_f.astype(BF16)
    rem = log_f - hi.astype(F32)
    mid = rem.astype(BF16)
    lo = (rem - mid.astype(F32)).astype(BF16)
    zero = jnp.zeros((c, HEAD), BF16)
    pieces = [jnp.concatenate([hi[x * c:(x + 1) * c], mid[x * c:(x + 1) * c], lo[x * c:(x + 1) * c], zero],
                              axis=0) for x in range(n_sub)]
    decays = []
    for x in range(0, n_sub - 1, 2):
        pair = jnp.exp(jnp.dot(p, jnp.concatenate(pieces[x:x + 2], axis=1), preferred_element_type=F32))
        decays += [pair[:, :HEAD], pair[:, HEAD:]]
    if n_sub % 2:
        decays.append(jnp.exp(jnp.dot(p, pieces[-1], preferred_element_type=F32)))

    def stack(per_sub):
        return per_sub[0] if n_sub == 1 else jnp.concatenate(per_sub, axis=0)

    def prod(factors):
        out = None
        for f in factors:
            out = f if out is None else out * f
        return out

    def scaled(block, factors):
        f = prod(factors)
        return block if f is None else block * f

    cum_in = [dx[n_low * c:(n_low + 1) * c] for dx in decays]
    cum_out = [dx[(n_low + 1) * c:] for dx in decays]
    whole = [e[c - 1:c, :] for e in cum_in]
    level_decay = [stack([dx[lvl * c:(lvl + 1) * c] for dx in decays]) for lvl in reversed(range(n_low))]
    g = 1
    while g < n_sub:
        per_sub = []
        for x in range(n_sub):
            mid_sub = x - x % (2 * g) + g
            if x >= mid_sub:
                per_sub.append(scaled(cum_in[x], whole[mid_sub:x]))
            else:
                per_sub.append(scaled(cum_out[x], whole[x + 1:mid_sub]))
        level_decay.append(stack(per_sub))
        g *= 2
    e_in = stack([scaled(cum_in[x], whole[:x]) for x in range(n_sub)])
    e_out = stack([scaled(cum_out[x], whole[x + 1:]) for x in range(n_sub)])

    t_idx = lax.broadcasted_iota(jnp.int32, (rows_all, rows_all), 0)
    s_idx = lax.broadcasted_iota(jnp.int32, (rows_all, rows_all), 1)
    x_idx = t_idx ^ s_idx
    scores = jnp.zeros((rows_all, rows_all), F32)
    for lvl, e in enumerate(level_decay):
        s_l = lax.dot_general((q * e).astype(BF16), (k * e).astype(BF16), _NT, preferred_element_type=F32)
        scores = jnp.where(x_idx >= (1 << lvl), s_l, scores)
    diag = jnp.where(t_idx == s_idx, jnp.sum(q * k, axis=-1, keepdims=True), 0.0)
    scores = jnp.where(t_idx > s_idx, scores, diag)

    o = jnp.dot(scores.astype(BF16), v16, preferred_element_type=F32)
    o = o + lax.dot_general((q * e_in).astype(BF16), st.astype(BF16), _NT, preferred_element_type=F32)
    st = st * prod(whole) + lax.dot_general(v16, (k * e_out).astype(BF16), _TN, preferred_element_type=F32)
    o = o * lax.rsqrt(jnp.mean(o * o, axis=-1, keepdims=True) + EPS) * gn
    return o * gate, st


def _hgrn_kernel(*refs, layer, chunk, n_sub, groups_per_iter, n_iters, heads_per_step):
    q_ref, f_ref, i_ref, g_ref, lbl_ref, gn_ref, p_ref, s0_ref = refs[:8]
    o_ref, s_ref, st_ref = refs[-3:]
    step = pl.program_id(2)

    @pl.when(step == 0)
    def _():
        for hh in range(heads_per_step):
            st_ref[hh] = s0_ref[0, hh].T

    logits = lbl_ref[...]
    ex = jnp.exp(logits - jnp.max(logits, axis=0, keepdims=True))
    sm = ex / jnp.sum(ex, axis=0, keepdims=True)
    lb = jnp.zeros_like(sm[0:1])
    for r in range(1, layer + 1):
        lb = lb + sm[r:r + 1]
    gn = gn_ref[...]
    p = p_ref[...]
    group = n_sub * chunk

    def body(it, carry):
        for hh in range(heads_per_step):
            lanes = slice(hh * HEAD, (hh + 1) * HEAD)
            st = st_ref[hh]
            for g in range(groups_per_iter):
                start = (it * groups_per_iter + g) * group
                rows = pl.ds(start if isinstance(start, int) else pl.multiple_of(start, group), group)
                o, st = _hgrn_block(q_ref[rows, lanes], f_ref[rows, lanes], i_ref[rows, lanes],
                                    g_ref[rows, lanes], lb[:, lanes], gn[:, lanes], p, st, chunk, n_sub)
                o_ref[rows, lanes] = o.astype(o_ref.dtype)
            st_ref[hh] = st
        return carry

    if n_iters == 1:
        body(0, 0)
    else:
        lax.fori_loop(0, n_iters, body, 0)

    @pl.when(step == pl.num_programs(2) - 1)
    def _():
        for hh in range(heads_per_step):
            s_ref[0, hh] = st_ref[hh].T.astype(s_ref.dtype)


def hgrn_mixer(z, lb_logits, gnorm, s0, layer, row0, seq, chunk, rows_per_step, n_sub, groups_per_iter,
               heads_per_step, dest):
    d = z.shape[1] // 4
    width = heads_per_step * HEAD
    groups = d // width
    batch = s0.shape[0]
    steps = seq // rows_per_step
    blk0 = row0 // rows_per_step
    rows_per_iter = chunk * n_sub * groups_per_iter
    assert row0 % rows_per_step == 0 and seq % rows_per_step == 0 and rows_per_step % rows_per_iter == 0
    p = jnp.asarray(_hgrn_sum_matrix(chunk), BF16)

    def col(part):
        return pl.BlockSpec((rows_per_step, width), lambda b, h, s: (blk0 + b * steps + s, part * groups + h))

    head_row = lambda b, h, s: (0, h)
    state_blk = pl.BlockSpec((1, heads_per_step, HEAD, HEAD), lambda b, h, s: (b, h, 0, 0))
    out_shape, dest_spec = _mixer_out(dest, d)
    return pl.pallas_call(
        functools.partial(_hgrn_kernel, layer=layer, chunk=chunk, n_sub=n_sub, groups_per_iter=groups_per_iter,
                          n_iters=rows_per_step // rows_per_iter, heads_per_step=heads_per_step),
        grid=(batch, groups, steps),
        in_specs=[col(0), col(1), col(2), col(3),
                  pl.BlockSpec((lb_logits.shape[0], width), head_row),
                  pl.BlockSpec((1, width), head_row),
                  pl.BlockSpec(p.shape, lambda b, h, s: (0, 0)),
                  state_blk, dest_spec],
        out_specs=[pl.BlockSpec((rows_per_step, width), lambda b, h, s: (blk0 + b * steps + s, h)), state_blk],
        out_shape=[out_shape, jax.ShapeDtypeStruct(s0.shape, F32)],
        input_output_aliases={8: 0},
        scratch_shapes=[pltpu.VMEM((heads_per_step, HEAD, HEAD), F32)],
        compiler_params=_params("arbitrary", "arbitrary", "arbitrary"),
        name="hgrn_mixer",
    )(z, z, z, z, lb_logits, gnorm.reshape(1, d), p, s0, dest)


def _later_keys_matrix(tk):
    j = lax.broadcasted_iota(jnp.int32, (tk, tk), 0)
    s = lax.broadcasted_iota(jnp.int32, (tk, tk), 1)
    return jnp.where(j > s, 1.0, 0.0).astype(BF16)


def _sb_weights(z, later, run, key_offset, rows_per_head=None):
    z = z.astype(BF16)
    log_beta = jnp.minimum(z, 0) - jnp.log(1 + jnp.exp(-jnp.abs(z)))
    log_keep = log_beta - z
    if key_offset is not None:
        t_idx = lax.broadcasted_iota(jnp.int32, z.shape, 0)
        if rows_per_head is not None:
            t_idx = lax.rem(t_idx, rows_per_head)
        mask = lax.broadcasted_iota(jnp.int32, z.shape, 1) + key_offset < t_idx
        log_keep = jnp.where(mask, log_keep, jnp.zeros_like(log_keep))
    after = jnp.dot(log_keep, later, preferred_element_type=F32)
    w = jnp.exp(log_beta.astype(F32) + after + run)
    if key_offset is not None:
        w = jnp.where(mask, w, 0.0)
    return w.astype(BF16), run + jnp.sum(log_keep.astype(F32), axis=-1, keepdims=True)


def _sb_tile(q, ks, vs, later, acc, run, key_offset):
    w, run = _sb_weights(lax.dot_general(q, ks, _NT, preferred_element_type=F32), later, run, key_offset)
    return acc + jnp.dot(w, vs, preferred_element_type=F32), run


def _sb_prompt_kernel(q_ref, k_ref, v_ref, dest_ref, o_ref, *, tq, tk, heads_per_step):
    del dest_ref
    qi = pl.program_id(2)
    later = _later_keys_matrix(tk)
    per_q = tq // tk
    lanes = [slice(hh * HEAD, (hh + 1) * HEAD) for hh in range(heads_per_step)]
    qs = [q_ref[:, ln] for ln in lanes]

    def visit(j, carry, key_offset):
        rows = pl.ds(j * tk if isinstance(j, int) else pl.multiple_of(j * tk, tk), tk)
        return tuple(_sb_tile(q, k_ref[rows, ln], v_ref[rows, ln], later, c[0], c[1], key_offset)
                     for q, ln, c in zip(qs, lanes, carry))

    carry = tuple((jnp.zeros((tq, HEAD), F32), jnp.zeros((tq, 1), F32)) for _ in lanes)
    for dgl in reversed(range(per_q)):
        r0 = dgl * tk
        rows = pl.ds(pl.multiple_of((qi * per_q + dgl) * tk, tk), tk)
        new = []
        for q, ln, c in zip(qs, lanes, carry):
            acc, run = _sb_tile(q[r0:], k_ref[rows, ln], v_ref[rows, ln], later, c[0][r0:], c[1][r0:], 0)
            if r0:
                acc = jnp.concatenate([c[0][:r0], acc], axis=0)
                run = jnp.concatenate([c[1][:r0], run], axis=0)
            new.append((acc, run))
        carry = tuple(new)

    def quad(t, c):
        j = qi * per_q - 1 - 4 * t
        for u in range(4):
            c = visit(j - u, c, None)
        return c

    n_left = qi * per_q
    carry = lax.fori_loop(0, n_left // 4, quad, carry)
    carry = lax.cond(n_left % 4 != 0, lambda c: visit(0, visit(1, c, None), None), lambda c: c, carry)
    for ln, c in zip(lanes, carry):
        o_ref[:, ln] = c[0].astype(o_ref.dtype)


def sb_prompt(q, kv16, batch, seq, dest, tq=SB_QUERY_TILE, tk=SB_KEY_TILE, heads_per_step=4):
    d = q.shape[1]
    width = heads_per_step * HEAD
    groups = d // width
    nq = seq // tq
    assert seq % tq == 0 and tq == 2 * tk and d % width == 0
    out_shape, dest_spec = _mixer_out(dest, d)
    return pl.pallas_call(
        functools.partial(_sb_prompt_kernel, tq=tq, tk=tk, heads_per_step=heads_per_step),
        grid=(batch, groups, nq),
        in_specs=[pl.BlockSpec((tq, width), lambda b, h, i: (b * nq + i, h)),
                  pl.BlockSpec((seq, width), lambda b, h, i: (b, h)),
                  pl.BlockSpec((seq, width), lambda b, h, i: (b, groups + h)),
                  dest_spec],
        out_specs=pl.BlockSpec((tq, width), lambda b, h, i: (b * nq + i, h)),
        out_shape=out_shape,
        input_output_aliases={3: 0},
        compiler_params=_params("arbitrary", "arbitrary", "arbitrary"),
        name="sb_prompt",
    )(q, kv16, kv16, dest)


def _cache_rows_kernel(ck_hbm, cv_hbm, ok_ref, ov_ref, buf, sem, *, heads_per_step, groups):
    n = pl.program_id(0) * groups + pl.program_id(1)
    steps = pl.num_programs(0) * groups

    def copies(step, slot):
        b, g = step // groups, step % groups
        return [pltpu.make_async_copy(src.at[b, :, g * heads_per_step + hh, :], buf.at[slot, a, hh],
                                      sem.at[slot, a, hh])
                for a, src in enumerate((ck_hbm, cv_hbm)) for hh in range(heads_per_step)]

    @pl.when(n == 0)
    def _():
        for c in copies(n, 0):
            c.start()

    @pl.when(n + 1 < steps)
    def _():
        for c in copies(n + 1, (n + 1) % 2):
            c.start()

    slot = n % 2
    for c in copies(n, slot):
        c.wait()
    for a, o_ref in enumerate((ok_ref, ov_ref)):
        for hh in range(heads_per_step):
            o_ref[:, hh * HEAD:(hh + 1) * HEAD] = buf[slot, a, hh].astype(o_ref.dtype)


def cache_rows(cache_k, cache_v, heads_per_step=SAMPLE_HEADS_PER_STEP):
    batch, past, heads, head = cache_k.shape
    assert head == HEAD and heads % heads_per_step == 0 and cache_v.shape == cache_k.shape
    groups = heads // heads_per_step
    out = jax.ShapeDtypeStruct((batch, past, heads * head), BF16)
    blk = pl.BlockSpec((None, past, heads_per_step * head), lambda b, g: (b, 0, g))
    return pl.pallas_call(
        functools.partial(_cache_rows_kernel, heads_per_step=heads_per_step, groups=groups),
        grid=(batch, groups),
        in_specs=[pl.BlockSpec(memory_space=pl.ANY), pl.BlockSpec(memory_space=pl.ANY)],
        out_specs=[blk, blk],
        out_shape=[out, out],
        scratch_shapes=[pltpu.VMEM((2, 2, heads_per_step, past, head), F32),
                        pltpu.SemaphoreType.DMA((2, 2, heads_per_step))],
        compiler_params=_params("arbitrary", "arbitrary"),
        name="cache_rows",
    )(cache_k, cache_v)


def _sb_sample_kernel(q_ref, k_ref, v_ref, ck_ref, cv_ref, dest_ref, o_ref, *, tk, heads_per_step):
    del dest_ref
    t = q_ref.shape[0]
    past = ck_ref.shape[0]
    lanes = [slice(hh * HEAD, (hh + 1) * HEAD) for hh in range(heads_per_step)]
    qs = [q_ref[:, ln] for ln in lanes]

    def visit(k_tile, v_tile, later, accs, run, key_offset):
        z = jnp.concatenate([lax.dot_general(q, k_tile(ln), _NT, preferred_element_type=F32)
                             for q, ln in zip(qs, lanes)], axis=0)
        w, run = _sb_weights(z, later, run, key_offset, t)
        accs = [acc + jnp.dot(w[hh * t:(hh + 1) * t], v_tile(ln), preferred_element_type=F32)
                for hh, (acc, ln) in enumerate(zip(accs, lanes))]
        return accs, run

    accs = [jnp.zeros((t, HEAD), F32) for _ in lanes]
    run = jnp.zeros((heads_per_step * t, 1), F32)
    accs, run = visit(lambda ln: k_ref[:, ln], lambda ln: v_ref[:, ln], _later_keys_matrix(t), accs, run, 0)
    later = _later_keys_matrix(tk)
    for j in range(past // tk - 1, -1, -1):
        rows = slice(j * tk, (j + 1) * tk)
        accs, run = visit(lambda ln: ck_ref[rows, ln], lambda ln: cv_ref[rows, ln], later, accs, run, None)
    for acc, ln in zip(accs, lanes):
        o_ref[:, ln] = acc.astype(o_ref.dtype)


def sb_sample(q, kv16, cache_k, cache_v, row0, batch, seq, dest, tk=SB_KEY_TILE,
              heads_per_step=SAMPLE_HEADS_PER_STEP):
    d = q.shape[1]
    width = heads_per_step * HEAD
    groups = d // width
    past = cache_k.shape[1]
    blk0 = row0 // seq
    assert row0 % seq == 0 and past % tk == 0
    own = lambda off: pl.BlockSpec((seq, width), lambda b, h: (blk0 + b, off + h))
    cache_blk = pl.BlockSpec((None, past, width), lambda b, h: (b, 0, h))
    out_shape, dest_spec = _mixer_out(dest, d)
    return pl.pallas_call(
        functools.partial(_sb_sample_kernel, tk=tk, heads_per_step=heads_per_step),
        grid=(batch, groups),
        in_specs=[own(0), own(0), own(groups), cache_blk, cache_blk, dest_spec],
        out_specs=own(0),
        out_shape=out_shape,
        input_output_aliases={5: 0},
        compiler_params=_params("arbitrary", "arbitrary"),
        name="sb_sample",
    )(q, kv16, kv16, cache_k, cache_v, dest)


def kernel(x_prompt, x_sample, state_hgrn, cache_k, cache_v, norm_mix, norm_ffn, norm_kv, norm_out,
           a_w_in, a_lb_logits, a_gnorm, a_w_out, b_w_kv, b_w_q, b_w_out, ffn_w_gate_up, ffn_w_down):
    pb, ps, d = x_prompt.shape
    sb, ss, _ = x_sample.shape
    n_a = a_w_in.shape[0]
    depth = norm_mix.shape[0]
    heads = d // HEAD
    mp, ms = pb * ps, sb * ss
    m = mp + ms

    x = jnp.concatenate([x_prompt.reshape(mp, d), x_sample.reshape(ms, d)], axis=0)
    zero_state = jnp.zeros((pb, heads, HEAD, HEAD), F32)
    states_p, states_s = [], []
    kv32 = kv16 = ck16 = cv16 = h_kv = None
    o = jnp.zeros((m, d), BF16)
    h, rnorm = rmsnorm(x, norm_mix[0], BF16, 384), None
    for layer in range(depth):
        if layer < n_a:
            z = matmul(h, a_w_in, layer, rnorm, tn=512)
            o, s_p = hgrn_mixer(z, a_lb_logits, a_gnorm[layer], zero_state, layer, 0, ps,
                                HGRN_CHUNK, 64 * HGRN_CHUNK, 4, 16, 1, o)
            o, s_s = hgrn_mixer(z, a_lb_logits, a_gnorm[layer], state_hgrn[layer], layer, mp, ss,
                                ss, ss, 1, 1, SAMPLE_HEADS_PER_STEP, o)
            states_p.append(s_p)
            states_s.append(s_s)
            x, (h_ffn,), rnorm_ffn = matmul(o, a_w_out, layer, res=x, gains=(norm_ffn[layer],))
        else:
            j = layer - n_a
            if j == 0:
                kv32, kv16 = matmul(h_kv, b_w_kv, None, rnorm, out_dtypes=(F32, BF16), tn=512)
                ck16, cv16 = cache_rows(cache_k, cache_v)
            q = matmul(h, b_w_q, j, rnorm, out_dtypes=(BF16,), out_scale=SB_SCALE, tn=512)
            o = sb_prompt(q, kv16, pb, ps, o)
            o = sb_sample(q, kv16, ck16, cv16, mp, sb, ss, o)
            x, (h_ffn,), rnorm_ffn = matmul(o, b_w_out, j, res=x, gains=(norm_ffn[layer],))
        hidden, w_down16 = gate_up(h_ffn, ffn_w_gate_up, ffn_w_down, layer, rnorm_ffn)
        if layer + 1 == depth:
            x = matmul(hidden, w_down16, res=x, tm=1056)
        elif layer + 1 == n_a:
            x, (h, h_kv), rnorm = matmul(hidden, w_down16, res=x, gains=(norm_mix[layer + 1], norm_kv), tm=1056)
        else:
            x, (h,), rnorm = matmul(hidden, w_down16, res=x, gains=(norm_mix[layer + 1],), tm=1056)
    y_p = rmsnorm(x, norm_out, F32, 256, 0, mp)
    y_s = rmsnorm(x, norm_out, F32, 256, mp, ms)

    k_new, v_new = kv32[:, :d], kv32[:, d:]
    return (y_p.reshape(pb, ps, d), y_s.reshape(sb, ss, d),
            jnp.stack(states_p).astype(x.dtype), k_new[:mp].reshape(pb, ps, heads, HEAD),
            v_new[:mp].reshape(pb, ps, heads, HEAD),
            jnp.stack(states_s).astype(x.dtype), k_new[mp:].reshape(sb, ss, heads, HEAD),
            v_new[mp:].reshape(sb, ss, heads, HEAD))
```
